```python
import math
import jax, jax.numpy as jnp
from jax import lax
import numpy as np

D_MODEL = 1024
BATCH = 8
SEQ = 2048
DEPTH = 2

CONV_WIDTH = 1024
CONV_KERNEL = 31
SSM_WIDTH = 512
SSM_GROUP = 16
SSM_GROUPS = SSM_WIDTH // SSM_GROUP
SSM_STATE = 64
DT_MIN = 1e-3
DT_MAX = 1e-1
EVEN_IN = 3 * CONV_WIDTH + 2 * SSM_WIDTH
EVEN_MIX = CONV_WIDTH + SSM_WIDTH
ATTN_HEADS = 16
ATTN_HEAD_DIM = 64
ATTN_WIDTH = ATTN_HEADS * ATTN_HEAD_DIM
ODD_IN = 4 * ATTN_WIDTH
Q_BLOCK = 128
EPS = 1e-6

kernel_name = "hybrid_conformer_s5_stickbreaking_block"


def rmsnorm(x, g):
    xf = x.astype(jnp.float32)
    y = xf * lax.rsqrt(jnp.mean(xf * xf, axis=-1, keepdims=True) + EPS)
    return (y * g.astype(jnp.float32)).astype(x.dtype)


def layernorm(x, g, b):
    xf = x.astype(jnp.float32)
    mu = jnp.mean(xf, axis=-1, keepdims=True)
    var = jnp.mean(jnp.square(xf - mu), axis=-1, keepdims=True)
    y = (xf - mu) * lax.rsqrt(var + EPS)
    return (y * g.astype(jnp.float32) + b.astype(jnp.float32)).astype(x.dtype)


def adaln(c, w_ada, b_ada):
    mod = jnp.einsum('bd,de->be', jax.nn.silu(c), w_ada) + b_ada
    shift, scale, gate = jnp.split(mod, 3, axis=-1)
    return shift[:, None, :], scale[:, None, :], gate[:, None, :]


def conformer_conv(val, glu_gate, conv_w, conv_b, ln_g, ln_b):
    h = val * jax.nn.sigmoid(glu_gate)
    h = lax.conv_general_dilated(
        h, conv_w[:, None, :], window_strides=(1,),
        padding=[(CONV_KERNEL - 1, 0)],
        dimension_numbers=('NWC', 'WIO', 'NWC'),
        feature_group_count=h.shape[-1]) + conv_b
    return jax.nn.silu(layernorm(h, ln_g, ln_b))


def _ssm_combine(e1, e2):
    a1, b1 = e1
    a2, b2 = e2
    return a1 * a2, a2 * b1 + b2


def s5_branch(u, lam_re, lam_im, log_dt, b_re, b_im, c_re, c_im, d_skip, w_glu, b_glu):
    f32 = jnp.float32
    bsz, seq, _ = u.shape
    lam = lax.complex(lam_re.astype(f32), lam_im.astype(f32))
    dt = jnp.exp(log_dt.astype(f32))[:, None]
    a_bar = jnp.exp(lam * dt)
    bmat = lax.complex(b_re.astype(f32), b_im.astype(f32))
    b_bar = ((a_bar - 1.0) / lam)[..., None] * bmat
    uf = u.astype(f32)
    ug = uf.reshape(bsz, seq, SSM_GROUPS, SSM_GROUP)
    bu = lax.complex(jnp.einsum('blgi,gpi->blgp', ug, jnp.real(b_bar)),
                     jnp.einsum('blgi,gpi->blgp', ug, jnp.imag(b_bar)))
    a = jnp.broadcast_to(a_bar, bu.shape)
    _, state = lax.associative_scan(_ssm_combine, (a, bu), axis=1)
    y = (jnp.einsum('blgp,gip->blgi', jnp.real(state), c_re.astype(f32))
         - jnp.einsum('blgp,gip->blgi', jnp.imag(state), c_im.astype(f32)))
    y = y.reshape(bsz, seq, SSM_WIDTH) + d_skip.astype(f32) * uf
    y = jax.nn.gelu(y)
    y = y * jax.nn.sigmoid(y @ w_glu.astype(f32) + b_glu.astype(f32))
    return y.astype(u.dtype)


def stick_breaking_attention(q, k, v):
    seq = q.shape[2]
    scale = ATTN_HEAD_DIM ** -0.5
    outs = []
    for blk in range(seq // Q_BLOCK):
        q0 = blk * Q_BLOCK
        kl = q0 + Q_BLOCK
        z = jnp.einsum('bhqd,bhkd->bhqk', q[:, :, q0:kl], k[:, :, :kl]).astype(jnp.float32) * scale
        t_idx = q0 + jnp.arange(Q_BLOCK)[:, None]
        s_idx = jnp.arange(kl)[None, :]
        mask = s_idx < t_idx
        log_1m = jnp.where(mask, jax.nn.log_sigmoid(-z), 0.0)
        between = lax.cumsum(log_1m, axis=3, reverse=True) - log_1m
        w = jnp.where(mask, jnp.exp(jax.nn.log_sigmoid(z) + between), 0.0)
        outs.append(jnp.einsum('bhqk,bhkd->bhqd', w.astype(v.dtype), v[:, :, :kl]))
    return jnp.concatenate(outs, axis=2)


def even_layer(x, c, norm_g, w_ada, b_ada, w_in, conv_w, conv_b, conv_ln_g, conv_ln_b,
               lam_re, lam_im, log_dt, b_re, b_im, c_re, c_im, d_skip, w_glu, b_glu, w_out):
    shift, scale, gate = adaln(c, w_ada, b_ada)
    h = rmsnorm(x, norm_g) * (1.0 + scale) + shift
    proj = h @ w_in
    val_a, glu_a, gate_a, u_b, gate_b = jnp.split(
        proj, [CONV_WIDTH, 2 * CONV_WIDTH, 3 * CONV_WIDTH, 3 * CONV_WIDTH + SSM_WIDTH], axis=-1)
    y_a = conformer_conv(val_a, glu_a, conv_w, conv_b, conv_ln_g, conv_ln_b) * jax.nn.silu(gate_a)
    y_b = s5_branch(u_b, lam_re, lam_im, log_dt, b_re, b_im, c_re, c_im, d_skip, w_glu, b_glu) * jax.nn.silu(gate_b)
    y = jnp.concatenate([y_a, y_b], axis=-1) @ w_out
    return x + gate * y


def odd_layer(x, c, norm_g, w_ada, b_ada, w_in, w_out):
    bsz, seq, _ = x.shape
    shift, scale, gate = adaln(c, w_ada, b_ada)
    h = rmsnorm(x, norm_g) * (1.0 + scale) + shift
    q, k, v, g = jnp.split(h @ w_in, 4, axis=-1)
    heads = lambda t: t.reshape(bsz, seq, ATTN_HEADS, ATTN_HEAD_DIM).transpose(0, 2, 1, 3)
    o = stick_breaking_attention(heads(q), heads(k), heads(v))
    o = o.transpose(0, 2, 1, 3).reshape(bsz, seq, ATTN_WIDTH) * jax.nn.silu(g)
    return x + gate * (o @ w_out)


def setup_inputs(seed: int = 0) -> dict:
    key = jax.random.key(seed)
    ks = iter(jax.random.split(key, 32))
    f32 = jnp.float32
    nrm = lambda shape, s: jax.random.normal(next(ks), shape, f32) * s
    d = D_MODEL
    n_idx = jnp.arange(SSM_STATE, dtype=f32)[None, :]
    return {
        "x": nrm((BATCH, SEQ, d), 1.0),
        "c": nrm((BATCH, d), 1.0),
        "l0_norm_g": 1.0 + nrm((d,), 0.02),
        "l0_w_ada": nrm((d, 3 * d), d ** -0.5),
        "l0_b_ada": nrm((3 * d,), 0.02),
        "l0_w_in": nrm((d, EVEN_IN), d ** -0.5),
        "l0_conv_w": nrm((CONV_KERNEL, CONV_WIDTH), CONV_KERNEL ** -0.5),
        "l0_conv_b": nrm((CONV_WIDTH,), 0.02),
        "l0_conv_ln_g": 1.0 + nrm((CONV_WIDTH,), 0.02),
        "l0_conv_ln_b": nrm((CONV_WIDTH,), 0.02),
        "l0_ssm_lam_re": -0.5 + nrm((SSM_GROUPS, SSM_STATE), 0.01),
        "l0_ssm_lam_im": math.pi * n_idx + nrm((SSM_GROUPS, SSM_STATE), 0.01),
        "l0_ssm_log_dt": jax.random.uniform(next(ks), (SSM_GROUPS,), f32,
                                             math.log(DT_MIN), math.log(DT_MAX)),
        "l0_ssm_b_re": nrm((SSM_GROUPS, SSM_STATE, SSM_GROUP), (2 * SSM_GROUP) ** -0.5),
        "l0_ssm_b_im": nrm((SSM_GROUPS, SSM_STATE, SSM_GROUP), (2 * SSM_GROUP) ** -0.5),
        "l0_ssm_c_re": nrm((SSM_GROUPS, SSM_GROUP, SSM_STATE), (2 * SSM_STATE) ** -0.5),
        "l0_ssm_c_im": nrm((SSM_GROUPS, SSM_GROUP, SSM_STATE), (2 * SSM_STATE) ** -0.5),
        "l0_ssm_d": 1.0 + nrm((SSM_WIDTH,), 0.1),
        "l0_ssm_w_glu": nrm((SSM_WIDTH, SSM_WIDTH), SSM_WIDTH ** -0.5),
        "l0_ssm_b_glu": nrm((SSM_WIDTH,), 0.02),
        "l0_w_out": nrm((EVEN_MIX, d), EVEN_MIX ** -0.5),
        "l1_norm_g": 1.0 + nrm((d,), 0.02),
        "l1_w_ada": nrm((d, 3 * d), d ** -0.5),
        "l1_b_ada": nrm((3 * d,), 0.02),
        "l1_w_in": nrm((d, ODD_IN), d ** -0.5),
        "l1_w_out": nrm((ATTN_WIDTH, d), ATTN_WIDTH ** -0.5),
        "final_norm_g": 1.0 + nrm((d,), 0.02),
    }


def reference(x, c, l0_norm_g, l0_w_ada, l0_b_ada, l0_w_in, l0_conv_w, l0_conv_b, l0_conv_ln_g,
              l0_conv_ln_b, l0_ssm_lam_re, l0_ssm_lam_im, l0_ssm_log_dt, l0_ssm_b_re, l0_ssm_b_im,
              l0_ssm_c_re, l0_ssm_c_im, l0_ssm_d, l0_ssm_w_glu, l0_ssm_b_glu, l0_w_out,
              l1_norm_g, l1_w_ada, l1_b_ada, l1_w_in, l1_w_out, final_norm_g):
    even_params = (l0_norm_g, l0_w_ada, l0_b_ada, l0_w_in, l0_conv_w, l0_conv_b, l0_conv_ln_g,
                   l0_conv_ln_b, l0_ssm_lam_re, l0_ssm_lam_im, l0_ssm_log_dt, l0_ssm_b_re,
                   l0_ssm_b_im, l0_ssm_c_re, l0_ssm_c_im, l0_ssm_d, l0_ssm_w_glu, l0_ssm_b_glu,
                   l0_w_out)
    odd_params = (l1_norm_g, l1_w_ada, l1_b_ada, l1_w_in, l1_w_out)
    for layer in range(DEPTH):
        if layer % 2 == 0:
            x = even_layer(x, c, *even_params)
        else:
            x = odd_layer(x, c, *odd_params)
    return rmsnorm(x, final_norm_g)
```

```python
import functools
import math

import jax
import jax.numpy as jnp
from jax import lax
from jax.experimental import pallas as pl
from jax.experimental.pallas import tpu as pltpu

F32 = jnp.float32
BF16 = jnp.bfloat16

D_MODEL = 1024
CONV_WIDTH = 1024
CONV_KERNEL = 31
SSM_WIDTH = 512
SSM_GROUP = 16
SSM_GROUPS = SSM_WIDTH // SSM_GROUP
SSM_STATE = 64
SSM_LANES = SSM_GROUPS * SSM_STATE
ATTN_HEADS = 16
ATTN_HEAD_DIM = 64
ATTN_WIDTH = ATTN_HEADS * ATTN_HEAD_DIM
Q_BLOCK = 128
EPS = 1e-6

SUBLANES = 8
LANES = 128
VMEM_LIMIT_BYTES = 56 * 1024 * 1024

ROW_TILE = 256
CONV_TILE = 128
CONV_HALO = 32
CONV_CHUNK = 32
SSM_CHUNK = 256
KEY_BLOCK = 128


def _silu(x):
    return x * jax.nn.sigmoid(x)


def _bdot(a, b):
    return jnp.dot(a.astype(BF16), b.astype(BF16), preferred_element_type=F32)


def _rms_modulate(x, g, scale, shift):
    ms = jnp.mean(x * x, axis=-1, keepdims=True)
    return (x * lax.rsqrt(ms + EPS) * g) * (1.0 + scale) + shift


def _ada_kernel(c_ref, w_ref, b_ref, o_ref):
    o_ref[...] = _bdot(_silu(c_ref[...]), w_ref[...]) + b_ref[...]


def _ada(c, w_ada, b_ada):
    bsz, d = c.shape
    n = w_ada.shape[1]
    tn = d
    return pl.pallas_call(
        _ada_kernel,
        grid=(n // tn,),
        in_specs=[pl.BlockSpec((bsz, d), lambda j: (0, 0)),
                  pl.BlockSpec((d, tn), lambda j: (0, j)),
                  pl.BlockSpec((1, tn), lambda j: (0, j))],
        out_specs=pl.BlockSpec((bsz, tn), lambda j: (0, j)),
        out_shape=jax.ShapeDtypeStruct((bsz, n), F32),
        compiler_params=pltpu.CompilerParams(dimension_semantics=("parallel",)),
    )(c, w_ada, b_ada.reshape(1, n))


def _even_in_kernel(x_ref, mod_ref, g_ref, w_ref, hg_ref, ga_ref, u_ref, gb_ref):
    mod = mod_ref[0]
    h = _rms_modulate(x_ref[0], g_ref[...], mod[1:2], mod[0:1]).astype(BF16)
    cw, sw = CONV_WIDTH, SSM_WIDTH
    val = jnp.dot(h, w_ref[:, 0:cw], preferred_element_type=F32)
    glu = jnp.dot(h, w_ref[:, cw:2 * cw], preferred_element_type=F32)
    hg_ref[0] = val * jax.nn.sigmoid(glu)
    ga_ref[0] = _silu(jnp.dot(h, w_ref[:, 2 * cw:3 * cw], preferred_element_type=F32))
    u_ref[0] = jnp.dot(h, w_ref[:, 3 * cw:3 * cw + sw], preferred_element_type=F32)
    gb_ref[0] = _silu(jnp.dot(h, w_ref[:, 3 * cw + sw:3 * cw + 2 * sw], preferred_element_type=F32))


def _even_in(x, mod, norm_g, w_in):
    bsz, seq, d = x.shape
    tl = ROW_TILE
    n_in = w_in.shape[1]
    row = lambda width: pl.BlockSpec((1, tl, width), lambda b, i: (b, i, 0))
    return pl.pallas_call(
        _even_in_kernel,
        grid=(bsz, seq // tl),
        in_specs=[row(d),
                  pl.BlockSpec((1, 3, d), lambda b, i: (b, 0, 0)),
                  pl.BlockSpec((1, d), lambda b, i: (0, 0)),
                  pl.BlockSpec((d, n_in), lambda b, i: (0, 0))],
        out_specs=[row(CONV_WIDTH), row(CONV_WIDTH), row(SSM_WIDTH), row(SSM_WIDTH)],
        out_shape=[jax.ShapeDtypeStruct((bsz, seq, CONV_WIDTH), F32),
                   jax.ShapeDtypeStruct((bsz, seq, CONV_WIDTH), F32),
                   jax.ShapeDtypeStruct((bsz, seq, SSM_WIDTH), F32),
                   jax.ShapeDtypeStruct((bsz, seq, SSM_WIDTH), F32)],
        compiler_params=pltpu.CompilerParams(
            dimension_semantics=("parallel", "parallel"), vmem_limit_bytes=VMEM_LIMIT_BYTES),
    )(x, mod, norm_g.reshape(1, d), w_in.astype(BF16))


def _conv_kernel(cur_ref, halo_ref, ga_ref, w_ref, cb_ref, lg_ref, lb_ref, o_ref, buf_ref):
    tl = cur_ref.shape[1]
    first = pl.program_id(1) == 0
    buf_ref[0:CONV_HALO, :] = jnp.where(first, 0.0, halo_ref[0])
    buf_ref[CONV_HALO:CONV_HALO + tl, :] = cur_ref[0]
    lead = CONV_HALO - (CONV_KERNEL - 1)

    for r0 in range(0, tl, CONV_CHUNK):
        acc = jnp.zeros((CONV_CHUNK, CONV_WIDTH), F32)
        for k in range(CONV_KERNEL):
            acc = acc + buf_ref[r0 + lead + k:r0 + lead + k + CONV_CHUNK, :] * w_ref[k:k + 1, :]
        acc = acc + cb_ref[...]
        mu = jnp.mean(acc, axis=-1, keepdims=True)
        cen = acc - mu
        var = jnp.mean(cen * cen, axis=-1, keepdims=True)
        y = cen * lax.rsqrt(var + EPS) * lg_ref[...] + lb_ref[...]
        o_ref[0, r0:r0 + CONV_CHUNK, :] = _silu(y) * ga_ref[0, r0:r0 + CONV_CHUNK, :]


def _conv_branch(hg, ga, conv_w, conv_b, ln_g, ln_b):
    bsz, seq, cw = hg.shape
    tl = CONV_TILE
    halo_per_tile = tl // CONV_HALO
    row = pl.BlockSpec((1, tl, cw), lambda b, i: (b, i, 0))
    vec = pl.BlockSpec((1, cw), lambda b, i: (0, 0))
    return pl.pallas_call(
        _conv_kernel,
        grid=(bsz, seq // tl),
        in_specs=[row,
                  pl.BlockSpec((1, CONV_HALO, cw),
                               lambda b, i: (b, jnp.maximum(i * halo_per_tile - 1, 0), 0)),
                  row,
                  pl.BlockSpec((CONV_KERNEL, cw), lambda b, i: (0, 0)),
                  vec, vec, vec],
        out_specs=row,
        out_shape=jax.ShapeDtypeStruct((bsz, seq, cw), F32),
        scratch_shapes=[pltpu.VMEM((CONV_HALO + tl, cw), F32)],
        compiler_params=pltpu.CompilerParams(dimension_semantics=("parallel", "parallel")),
    )(hg, hg, ga, conv_w, conv_b.reshape(1, cw), ln_g.reshape(1, cw), ln_b.reshape(1, cw))


def _ssm_prep_kernel(lre_ref, lim_ref, ldt_ref, bre_ref, bim_ref, pre_ref, pim_ref, bbre_ref, bbim_ref):
    lre, lim = lre_ref[...], lim_ref[...]
    dt = jnp.exp(ldt_ref[...])
    mag = jnp.exp(lre * dt)
    are, aim = mag * jnp.cos(lim * dt), mag * jnp.sin(lim * dt)
    nre, nim = are - 1.0, aim
    den = lre * lre + lim * lim
    cre = (nre * lre + nim * lim) / den
    cim = (nim * lre - nre * lim) / den
    bre, bim = bre_ref[...], bim_ref[...]
    bbre_ref[...] = cre * bre - cim * bim
    bbim_ref[...] = cre * bim + cim * bre
    pr, pi = are, aim
    pre_ref[0:1, :] = pr
    pim_ref[0:1, :] = pi
    for k in range(1, SUBLANES):
        pr, pi = pr * are - pi * aim, pr * aim + pi * are
        pre_ref[k:k + 1, :] = pr
        pim_ref[k:k + 1, :] = pi


def _ssm_prep(lam_re, lam_im, log_dt, b_re, b_im):
    n = SSM_LANES
    flat = lambda t: t.reshape(1, n)
    to_lanes = lambda t: t.reshape(n, SSM_GROUP).T
    ldt = jnp.broadcast_to(log_dt[:, None], (SSM_GROUPS, SSM_STATE))
    outs = pl.pallas_call(
        _ssm_prep_kernel,
        out_shape=[jax.ShapeDtypeStruct((SUBLANES, n), F32), jax.ShapeDtypeStruct((SUBLANES, n), F32),
                   jax.ShapeDtypeStruct((SSM_GROUP, n), F32), jax.ShapeDtypeStruct((SSM_GROUP, n), F32)],
    )(flat(lam_re), flat(lam_im), flat(ldt), to_lanes(b_re), to_lanes(b_im))
    return outs


def _group_mask_in():
    r = jnp.arange(SSM_WIDTH)[:, None] // SSM_GROUP
    c = jnp.arange(SSM_LANES)[None, :] // SSM_STATE
    return r == c


def _block_diag_in(bb):
    return jnp.where(_group_mask_in(), jnp.tile(bb, (SSM_GROUPS, 1)), 0.0).astype(BF16)


def _block_diag_out(cmat):
    per_lane = cmat.transpose(0, 2, 1).reshape(SSM_LANES, SSM_GROUP)
    return jnp.where(_group_mask_in().T, jnp.tile(per_lane, (1, SSM_GROUPS)), 0.0).astype(BF16)


def _ssm_kernel(u_ref, gb_ref, wbre_ref, wbim_ref, pre_ref, pim_ref, wcre_ref, wcim_ref,
                d_ref, wg_ref, bg_ref, o_ref, hre_ref, him_ref, cre_ref, cim_ref):
    tc = u_ref.shape[1]

    @pl.when(pl.program_id(1) == 0)
    def _():
        cre_ref[...] = jnp.zeros_like(cre_ref)
        cim_ref[...] = jnp.zeros_like(cim_ref)

    u = u_ref[0]
    ub = u.astype(BF16)
    hre_ref[...] = jnp.dot(ub, wbre_ref[...], preferred_element_type=F32)
    him_ref[...] = jnp.dot(ub, wbim_ref[...], preferred_element_type=F32)

    row = lax.broadcasted_iota(jnp.int32, (SUBLANES, LANES), 0)

    def cmul_add(xr, xi, ar, ai, sr, si):
        return xr + (ar * sr - ai * si), xi + (ar * si + ai * sr)

    for j in range(SSM_LANES // LANES):
        ls = slice(j * LANES, (j + 1) * LANES)
        pr, pi = pre_ref[:, ls], pim_ref[:, ls]
        bc = lambda v, k: jnp.broadcast_to(v[k:k + 1, :], (SUBLANES, LANES))
        steps = [(s, jnp.where(row >= s, bc(pr, s - 1), 0.0), jnp.where(row >= s, bc(pi, s - 1), 0.0))
                 for s in (1, 2, 4)]

        def block(r, carry):
            cr, ci = carry
            r0 = pl.multiple_of(r * SUBLANES, SUBLANES)
            xr = hre_ref[pl.ds(r0, SUBLANES), ls]
            xi = him_ref[pl.ds(r0, SUBLANES), ls]
            for s, ar, ai in steps:
                xr, xi = cmul_add(xr, xi, ar, ai, pltpu.roll(xr, s, 0), pltpu.roll(xi, s, 0))
            xr, xi = cmul_add(xr, xi, pr, pi, cr, ci)
            hre_ref[pl.ds(r0, SUBLANES), ls] = xr
            him_ref[pl.ds(r0, SUBLANES), ls] = xi
            last = SUBLANES - 1
            return bc(xr, last), bc(xi, last)

        carry0 = (jnp.broadcast_to(cre_ref[:, ls], (SUBLANES, LANES)),
                  jnp.broadcast_to(cim_ref[:, ls], (SUBLANES, LANES)))
        cr, ci = lax.fori_loop(0, tc // SUBLANES, block, carry0, unroll=4)
        cre_ref[:, ls] = cr[0:1, :]
        cim_ref[:, ls] = ci[0:1, :]

    y = (jnp.dot(hre_ref[...].astype(BF16), wcre_ref[...], preferred_element_type=F32)
         - jnp.dot(him_ref[...].astype(BF16), wcim_ref[...], preferred_element_type=F32))
    y = y + d_ref[...] * u
    y = jax.nn.gelu(y)
    y = y * jax.nn.sigmoid(_bdot(y, wg_ref[...]) + bg_ref[...])
    o_ref[0] = y * gb_ref[0]


def _ssm_branch(u, gb, wbre, wbim, pre, pim, wcre, wcim, d_skip, w_glu, b_glu):
    bsz, seq, sw = u.shape
    tc = SSM_CHUNK
    n = SSM_LANES
    row = pl.BlockSpec((1, tc, sw), lambda b, i: (b, i, 0))
    full = lambda r, c: pl.BlockSpec((r, c), lambda b, i: (0, 0))
    return pl.pallas_call(
        _ssm_kernel,
        grid=(bsz, seq // tc),
        in_specs=[row, row, full(sw, n), full(sw, n), full(SUBLANES, n), full(SUBLANES, n),
                  full(n, sw), full(n, sw), full(1, sw), full(sw, sw), full(1, sw)],
        out_specs=row,
        out_shape=jax.ShapeDtypeStruct((bsz, seq, sw), F32),
        scratch_shapes=[pltpu.VMEM((tc, n), F32), pltpu.VMEM((tc, n), F32),
                        pltpu.VMEM((1, n), F32), pltpu.VMEM((1, n), F32)],
        compiler_params=pltpu.CompilerParams(
            dimension_semantics=("parallel", "arbitrary"), vmem_limit_bytes=VMEM_LIMIT_BYTES),
    )(u, gb, wbre, wbim, pre, pim, wcre, wcim, d_skip.reshape(1, sw), w_glu.astype(BF16),
      b_glu.reshape(1, sw))


def _mid_kernel(x_ref, ya_ref, yb_ref, mod0_ref, mod1_ref, g_ref, wo_ref, wi_ref,
                x1_ref, q_ref, k_ref, v_ref, sg_ref):
    cw, aw = CONV_WIDTH, ATTN_WIDTH
    y = (jnp.dot(ya_ref[0].astype(BF16), wo_ref[0:cw, :], preferred_element_type=F32)
         + jnp.dot(yb_ref[0].astype(BF16), wo_ref[cw:, :], preferred_element_type=F32))
    x1 = x_ref[0] + mod0_ref[0][2:3] * y
    x1_ref[0] = x1
    mod1 = mod1_ref[0]
    h = _rms_modulate(x1, g_ref[...], mod1[1:2], mod1[0:1]).astype(BF16)
    scale = ATTN_HEAD_DIM ** -0.5
    q_ref[0] = (jnp.dot(h, wi_ref[:, 0:aw], preferred_element_type=F32) * scale).astype(BF16)
    k_ref[0] = jnp.dot(h, wi_ref[:, aw:2 * aw], preferred_element_type=F32).astype(BF16)
    v_ref[0] = jnp.dot(h, wi_ref[:, 2 * aw:3 * aw], preferred_element_type=F32).astype(BF16)
    sg_ref[0] = _silu(jnp.dot(h, wi_ref[:, 3 * aw:4 * aw], preferred_element_type=F32))


def _mid(x, ya, yb, mod0, mod1, norm_g, w_out, w_in):
    bsz, seq, d = x.shape
    tl = ROW_TILE
    row = lambda width: pl.BlockSpec((1, tl, width), lambda b, i: (b, i, 0))
    modspec = pl.BlockSpec((1, 3, d), lambda b, i: (b, 0, 0))
    full = lambda r, c: pl.BlockSpec((r, c), lambda b, i: (0, 0))
    aw = ATTN_WIDTH
    return pl.pallas_call(
        _mid_kernel,
        grid=(bsz, seq // tl),
        in_specs=[row(d), row(CONV_WIDTH), row(SSM_WIDTH), modspec, modspec, full(1, d),
                  full(CONV_WIDTH + SSM_WIDTH, d), full(d, 4 * aw)],
        out_specs=[row(d), row(aw), row(aw), row(aw), row(aw)],
        out_shape=[jax.ShapeDtypeStruct((bsz, seq, d), F32),
                   jax.ShapeDtypeStruct((bsz, seq, aw), BF16),
                   jax.ShapeDtypeStruct((bsz, seq, aw), BF16),
                   jax.ShapeDtypeStruct((bsz, seq, aw), BF16),
                   jax.ShapeDtypeStruct((bsz, seq, aw), F32)],
        compiler_params=pltpu.CompilerParams(
            dimension_semantics=("parallel", "parallel"), vmem_limit_bytes=VMEM_LIMIT_BYTES),
    )(x, ya, yb, mod0, mod1, norm_g.reshape(1, d), w_out.astype(BF16), w_in.astype(BF16))


def _attn_kernel(q_ref, k_ref, v_ref, o_ref):
    qb, kb = Q_BLOCK, KEY_BLOCK
    qi = pl.program_id(2)
    q = q_ref[0].astype(F32)
    lane = lax.broadcasted_iota(jnp.int32, (qb, 2 * ATTN_HEAD_DIM), 1)
    q_heads = (jnp.where(lane < ATTN_HEAD_DIM, q, 0.0).astype(BF16),
               jnp.where(lane >= ATTN_HEAD_DIM, q, 0.0).astype(BF16))
    r_idx = lax.broadcasted_iota(jnp.int32, (kb, kb), 0)
    c_idx = lax.broadcasted_iota(jnp.int32, (kb, kb), 1)
    later = (r_idx > c_idx).astype(BF16)
    causal = c_idx < r_idx

    def visit(start, diagonal, state):
        k = k_ref[0, pl.ds(start, kb), :]
        v = v_ref[0, pl.ds(start, kb), :]
        new_state = []
        for qh, (carry, acc) in zip(q_heads, state):
            z = lax.dot_general(qh, k, (((1,), (1,)), ((), ())), preferred_element_type=F32)
            log_beta = jnp.minimum(z, 0.0) - jnp.log(1.0 + jnp.exp(-jnp.abs(z)))
            log_1m = log_beta - z
            if diagonal:
                log_1m = jnp.where(causal, log_1m, 0.0)
            hi = log_1m.astype(BF16)
            lo = (log_1m - hi.astype(F32)).astype(BF16)
            between = (jnp.dot(hi, later, preferred_element_type=F32)
                       + jnp.dot(lo, later, preferred_element_type=F32))
            w = jnp.exp(log_beta + between + carry)
            if diagonal:
                w = jnp.where(causal, w, 0.0)
            acc = acc + jnp.dot(w.astype(BF16), v, preferred_element_type=F32)
            carry = carry + jnp.sum(log_1m, axis=1, keepdims=True)
            new_state.append((carry, acc))
        return tuple(new_state)

    zero = (jnp.zeros((qb, 1), F32), jnp.zeros((qb, 2 * ATTN_HEAD_DIM), F32))
    state = visit(pl.multiple_of(qi * qb, qb), True, (zero, zero))

    def body(j, state):
        return visit(pl.multiple_of((qi - 1 - j) * kb, kb), False, state)

    state = lax.fori_loop(0, qi * (qb // kb), body, state)
    o_ref[0] = jnp.where(lane < ATTN_HEAD_DIM, state[0][1], state[1][1])


def _attention(q, k, v):
    bsz, seq, aw = q.shape
    pair = 2 * ATTN_HEAD_DIM
    return pl.pallas_call(
        _attn_kernel,
        grid=(bsz, aw // pair, seq // Q_BLOCK),
        in_specs=[pl.BlockSpec((1, Q_BLOCK, pair), lambda b, h, i: (b, i, h)),
                  pl.BlockSpec((1, seq, pair), lambda b, h, i: (b, 0, h)),
                  pl.BlockSpec((1, seq, pair), lambda b, h, i: (b, 0, h))],
        out_specs=pl.BlockSpec((1, Q_BLOCK, pair), lambda b, h, i: (b, i, h)),
        out_shape=jax.ShapeDtypeStruct((bsz, seq, aw), F32),
        compiler_params=pltpu.CompilerParams(
            dimension_semantics=("parallel", "parallel", "parallel")),
    )(q, k, v)


def _final_kernel(x_ref, o_ref, sg_ref, mod_ref, wo_ref, g_ref, out_ref):
    y = _bdot(o_ref[0] * sg_ref[0], wo_ref[...])
    x2 = x_ref[0] + mod_ref[0][2:3] * y
    ms = jnp.mean(x2 * x2, axis=-1, keepdims=True)
    out_ref[0] = x2 * lax.rsqrt(ms + EPS) * g_ref[...]


def _final(x1, o, sg, mod1, w_out, final_g):
    bsz, seq, d = x1.shape
    tl = ROW_TILE
    row = pl.BlockSpec((1, tl, d), lambda b, i: (b, i, 0))
    return pl.pallas_call(
        _final_kernel,
        grid=(bsz, seq // tl),
        in_specs=[row, row, row,
                  pl.BlockSpec((1, 3, d), lambda b, i: (b, 0, 0)),
                  pl.BlockSpec((ATTN_WIDTH, d), lambda b, i: (0, 0)),
                  pl.BlockSpec((1, d), lambda b, i: (0, 0))],
        out_specs=row,
        out_shape=jax.ShapeDtypeStruct((bsz, seq, d), F32),
        compiler_params=pltpu.CompilerParams(dimension_semantics=("parallel", "parallel")),
    )(x1, o, sg, mod1, w_out.astype(BF16), final_g.reshape(1, d))


def kernel(x, c, l0_norm_g, l0_w_ada, l0_b_ada, l0_w_in, l0_conv_w, l0_conv_b, l0_conv_ln_g, l0_conv_ln_b, l0_ssm_lam_re, l0_ssm_lam_im, l0_ssm_log_dt, l0_ssm_b_re, l0_ssm_b_im, l0_ssm_c_re, l0_ssm_c_im, l0_ssm_d, l0_ssm_w_glu, l0_ssm_b_glu, l0_w_out, l1_norm_g, l1_w_ada, l1_b_ada, l1_w_in, l1_w_out, final_norm_g):
    bsz, seq, d = x.shape
    mod0 = _ada(c, l0_w_ada, l0_b_ada).reshape(bsz, 3, d)
    mod1 = _ada(c, l1_w_ada, l1_b_ada).reshape(bsz, 3, d)

    hg, ga, u, gb = _even_in(x, mod0, l0_norm_g, l0_w_in)
    ya = _conv_branch(hg, ga, l0_conv_w, l0_conv_b, l0_conv_ln_g, l0_conv_ln_b)

    pre, pim, bbre, bbim = _ssm_prep(l0_ssm_lam_re, l0_ssm_lam_im, l0_ssm_log_dt, l0_ssm_b_re, l0_ssm_b_im)
    yb = _ssm_branch(u, gb, _block_diag_in(bbre), _block_diag_in(bbim), pre, pim,
                     _block_diag_out(l0_ssm_c_re), _block_diag_out(l0_ssm_c_im),
                     l0_ssm_d, l0_ssm_w_glu, l0_ssm_b_glu)

    x1, q, k, v, sg = _mid(x, ya, yb, mod0, mod1, l1_norm_g, l0_w_out, l1_w_in)
    o = _attention(q, k, v)
    return _final(x1, o, sg, mod1, l1_w_out, final_norm_g)
```

```python
import math

import jax
import jax.numpy as jnp
from jax import lax
from jax.experimental import pallas as pl
from jax.experimental.pallas import tpu as pltpu

F32 = jnp.float32
BF16 = jnp.bfloat16

D_MODEL = 1024
CONV_WIDTH = 1024
CONV_KERNEL = 31
SSM_WIDTH = 512
SSM_GROUP = 16
SSM_GROUPS = SSM_WIDTH // SSM_GROUP
SSM_STATE = 64
SSM_LANES = SSM_GROUPS * SSM_STATE
ATTN_HEADS = 16
ATTN_HEAD_DIM = 64
ATTN_WIDTH = ATTN_HEADS * ATTN_HEAD_DIM
EPS = 1e-6
LOG2_E = math.log2(math.e)
MASKED_LOGIT = 1e30

SUBLANES = 8
LANES = 128
VMEM_LIMIT_BYTES = 56 * 1024 * 1024

ROW_TILE = 256
CONV_TILE = 128
CONV_HALO = 32
CONV_CHUNK = 32
SSM_CHUNK = 256
SSM_SCAN_TILES = 4
ATTN_BLOCK = 256
ATTN_PAIRS = 2


def _silu(x):
    return x * jax.nn.sigmoid(x)


def _bdot(a, b):
    return jnp.dot(a.astype(BF16), b.astype(BF16), preferred_element_type=F32)


def _rms_modulate(x, g, scale, shift):
    ms = jnp.mean(x * x, axis=-1, keepdims=True)
    return (x * lax.rsqrt(ms + EPS) * g) * (1.0 + scale) + shift


def _ada_kernel(c_ref, w_ref, b_ref, o_ref):
    o_ref[...] = _bdot(_silu(c_ref[...]), w_ref[...]) + b_ref[...]


def _ada(c, w_ada, b_ada):
    bsz, d = c.shape
    n = w_ada.shape[1]
    tn = d
    return pl.pallas_call(
        _ada_kernel,
        grid=(n // tn,),
        in_specs=[pl.BlockSpec((bsz, d), lambda j: (0, 0)),
                  pl.BlockSpec((d, tn), lambda j: (0, j)),
                  pl.BlockSpec((1, tn), lambda j: (0, j))],
        out_specs=pl.BlockSpec((bsz, tn), lambda j: (0, j)),
        out_shape=jax.ShapeDtypeStruct((bsz, n), F32),
        compiler_params=pltpu.CompilerParams(dimension_semantics=("parallel",)),
    )(c, w_ada, b_ada.reshape(1, n))


def _even_in_kernel(x_ref, mod_ref, g_ref, w_ref, hg_ref, ga_ref, u_ref, gb_ref):
    mod = mod_ref[0]
    h = _rms_modulate(x_ref[0], g_ref[...], mod[1:2], mod[0:1]).astype(BF16)
    cw, sw = CONV_WIDTH, SSM_WIDTH
    val = jnp.dot(h, w_ref[:, 0:cw], preferred_element_type=F32)
    glu = jnp.dot(h, w_ref[:, cw:2 * cw], preferred_element_type=F32)
    hg_ref[0] = val * jax.nn.sigmoid(glu)
    ga_ref[0] = _silu(jnp.dot(h, w_ref[:, 2 * cw:3 * cw], preferred_element_type=F32))
    u_ref[0] = jnp.dot(h, w_ref[:, 3 * cw:3 * cw + sw], preferred_element_type=F32)
    gb_ref[0] = _silu(jnp.dot(h, w_ref[:, 3 * cw + sw:3 * cw + 2 * sw], preferred_element_type=F32))


def _even_in(x, mod, norm_g, w_in):
    bsz, seq, d = x.shape
    tl = ROW_TILE
    n_in = w_in.shape[1]
    row = lambda width: pl.BlockSpec((1, tl, width), lambda b, i: (b, i, 0))
    return pl.pallas_call(
        _even_in_kernel,
        grid=(bsz, seq // tl),
        in_specs=[row(d),
                  pl.BlockSpec((1, 3, d), lambda b, i: (b, 0, 0)),
                  pl.BlockSpec((1, d), lambda b, i: (0, 0)),
                  pl.BlockSpec((d, n_in), lambda b, i: (0, 0))],
        out_specs=[row(CONV_WIDTH), row(CONV_WIDTH), row(SSM_WIDTH), row(SSM_WIDTH)],
        out_shape=[jax.ShapeDtypeStruct((bsz, seq, CONV_WIDTH), F32),
                   jax.ShapeDtypeStruct((bsz, seq, CONV_WIDTH), F32),
                   jax.ShapeDtypeStruct((bsz, seq, SSM_WIDTH), F32),
                   jax.ShapeDtypeStruct((bsz, seq, SSM_WIDTH), F32)],
        compiler_params=pltpu.CompilerParams(
            dimension_semantics=("parallel", "parallel"), vmem_limit_bytes=VMEM_LIMIT_BYTES),
    )(x, mod, norm_g.reshape(1, d), w_in.astype(BF16))


def _conv_kernel(cur_ref, halo_ref, ga_ref, w_ref, cb_ref, lg_ref, lb_ref, o_ref, buf_ref, sh_ref):
    tl = cur_ref.shape[1]
    first = pl.program_id(1) == 0
    buf_ref[0:CONV_HALO, :] = jnp.where(first, 0.0, halo_ref[0])
    buf_ref[CONV_HALO:CONV_HALO + tl, :] = cur_ref[0]
    lead = CONV_HALO - (CONV_KERNEL - 1)
    rows = sh_ref.shape[1]
    for r in range(1, SUBLANES):
        sh_ref[r - 1] = buf_ref[r:r + rows, :]

    for r0 in range(0, tl, CONV_CHUNK):
        acc = jnp.zeros((CONV_CHUNK, CONV_WIDTH), F32)
        for k in range(CONV_KERNEL):
            r, base = (lead + k) % SUBLANES, r0 + (lead + k) // SUBLANES * SUBLANES
            src = buf_ref if r == 0 else sh_ref.at[r - 1]
            acc = acc + src[base:base + CONV_CHUNK, :] * w_ref[k:k + 1, :]
        acc = acc + cb_ref[...]
        mu = jnp.mean(acc, axis=-1, keepdims=True)
        cen = acc - mu
        var = jnp.mean(cen * cen, axis=-1, keepdims=True)
        y = cen * lax.rsqrt(var + EPS) * lg_ref[...] + lb_ref[...]
        o_ref[0, r0:r0 + CONV_CHUNK, :] = _silu(y) * ga_ref[0, r0:r0 + CONV_CHUNK, :]


def _conv_branch(hg, ga, conv_w, conv_b, ln_g, ln_b):
    bsz, seq, cw = hg.shape
    tl = CONV_TILE
    halo_per_tile = tl // CONV_HALO
    row = pl.BlockSpec((1, tl, cw), lambda b, i: (b, i, 0))
    vec = pl.BlockSpec((1, cw), lambda b, i: (0, 0))
    return pl.pallas_call(
        _conv_kernel,
        grid=(bsz, seq // tl),
        in_specs=[row,
                  pl.BlockSpec((1, CONV_HALO, cw),
                               lambda b, i: (b, jnp.maximum(i * halo_per_tile - 1, 0), 0)),
                  row,
                  pl.BlockSpec((CONV_KERNEL, cw), lambda b, i: (0, 0)),
                  vec, vec, vec],
        out_specs=row,
        out_shape=jax.ShapeDtypeStruct((bsz, seq, cw), F32),
        scratch_shapes=[pltpu.VMEM((CONV_HALO + tl, cw), F32),
                        pltpu.VMEM((SUBLANES - 1, CONV_HALO + tl - SUBLANES, cw), F32)],
        compiler_params=pltpu.CompilerParams(dimension_semantics=("parallel", "parallel")),
    )(hg, hg, ga, conv_w, conv_b.reshape(1, cw), ln_g.reshape(1, cw), ln_b.reshape(1, cw))


def _ssm_prep_kernel(lre_ref, lim_ref, ldt_ref, bre_ref, bim_ref, pre_ref, pim_ref, bbre_ref, bbim_ref):
    lre, lim = lre_ref[...], lim_ref[...]
    dt = jnp.exp(ldt_ref[...])
    mag = jnp.exp(lre * dt)
    are, aim = mag * jnp.cos(lim * dt), mag * jnp.sin(lim * dt)
    nre, nim = are - 1.0, aim
    den = lre * lre + lim * lim
    cre = (nre * lre + nim * lim) / den
    cim = (nim * lre - nre * lim) / den
    bre, bim = bre_ref[...], bim_ref[...]
    bbre_ref[...] = cre * bre - cim * bim
    bbim_ref[...] = cre * bim + cim * bre
    pr, pi = are, aim
    pre_ref[0:1, :] = pr
    pim_ref[0:1, :] = pi
    for k in range(1, SUBLANES):
        pr, pi = pr * are - pi * aim, pr * aim + pi * are
        pre_ref[k:k + 1, :] = pr
        pim_ref[k:k + 1, :] = pi


def _ssm_prep(lam_re, lam_im, log_dt, b_re, b_im):
    n = SSM_LANES
    flat = lambda t: t.reshape(1, n)
    to_lanes = lambda t: t.reshape(n, SSM_GROUP).T
    ldt = jnp.broadcast_to(log_dt[:, None], (SSM_GROUPS, SSM_STATE))
    outs = pl.pallas_call(
        _ssm_prep_kernel,
        out_shape=[jax.ShapeDtypeStruct((SUBLANES, n), F32), jax.ShapeDtypeStruct((SUBLANES, n), F32),
                   jax.ShapeDtypeStruct((SSM_GROUP, n), F32), jax.ShapeDtypeStruct((SSM_GROUP, n), F32)],
    )(flat(lam_re), flat(lam_im), flat(ldt), to_lanes(b_re), to_lanes(b_im))
    return outs


def _group_mask_in():
    r = jnp.arange(SSM_WIDTH)[:, None] // SSM_GROUP
    c = jnp.arange(SSM_LANES)[None, :] // SSM_STATE
    return r == c


def _block_diag_in(bb):
    return jnp.where(_group_mask_in(), jnp.tile(bb, (SSM_GROUPS, 1)), 0.0).astype(BF16)


def _block_diag_out(cmat):
    per_lane = cmat.transpose(0, 2, 1).reshape(SSM_LANES, SSM_GROUP)
    return jnp.where(_group_mask_in().T, jnp.tile(per_lane, (1, SSM_GROUPS)), 0.0).astype(BF16)


def _ssm_kernel(u_ref, gb_ref, wbre_ref, wbim_ref, pre_ref, pim_ref, wcre_ref, wcim_ref,
                d_ref, wg_ref, bg_ref, o_ref, hre_ref, him_ref, cre_ref, cim_ref):
    tc = u_ref.shape[1]

    @pl.when(pl.program_id(1) == 0)
    def _():
        cre_ref[...] = jnp.zeros_like(cre_ref)
        cim_ref[...] = jnp.zeros_like(cim_ref)

    u = u_ref[0]
    ub = u.astype(BF16)
    hw, hn = SSM_WIDTH // 2, SSM_LANES // 2
    for half in range(2):
        rows, lns = slice(half * hw, (half + 1) * hw), slice(half * hn, (half + 1) * hn)
        hre_ref[:, lns] = jnp.dot(ub[:, rows], wbre_ref[rows, lns], preferred_element_type=F32)
        him_ref[:, lns] = jnp.dot(ub[:, rows], wbim_ref[rows, lns], preferred_element_type=F32)

    row = lax.broadcasted_iota(jnp.int32, (SUBLANES, LANES), 0)
    last = SUBLANES - 1
    bc = lambda v, k: jnp.broadcast_to(v[k:k + 1, :], (SUBLANES, LANES))

    def cmul_add(xr, xi, ar, ai, sr, si):
        return xr + (ar * sr - ai * si), xi + (ar * si + ai * sr)

    for j0 in range(0, SSM_LANES // LANES, SSM_SCAN_TILES):
        tiles = [slice(j * LANES, (j + 1) * LANES) for j in range(j0, j0 + SSM_SCAN_TILES)]
        pows = [(pre_ref[:, ls], pim_ref[:, ls]) for ls in tiles]
        steps = [[(s, jnp.where(row >= s, bc(pr, s - 1), 0.0), jnp.where(row >= s, bc(pi, s - 1), 0.0))
                  for s in (1, 2, 4)] for pr, pi in pows]

        def block(r, carry):
            r0 = pl.multiple_of(r * SUBLANES, SUBLANES)
            out = []
            for ls, (pr, pi), st, (cr, ci) in zip(tiles, pows, steps, carry):
                xr = hre_ref[pl.ds(r0, SUBLANES), ls]
                xi = him_ref[pl.ds(r0, SUBLANES), ls]
                for s, ar, ai in st:
                    xr, xi = cmul_add(xr, xi, ar, ai, pltpu.roll(xr, s, 0), pltpu.roll(xi, s, 0))
                xr, xi = cmul_add(xr, xi, pr, pi, cr, ci)
                hre_ref[pl.ds(r0, SUBLANES), ls] = xr
                him_ref[pl.ds(r0, SUBLANES), ls] = xi
                out.append((bc(xr, last), bc(xi, last)))
            return tuple(out)

        carry0 = tuple((jnp.broadcast_to(cre_ref[:, ls], (SUBLANES, LANES)),
                        jnp.broadcast_to(cim_ref[:, ls], (SUBLANES, LANES))) for ls in tiles)
        carry = lax.fori_loop(0, tc // SUBLANES, block, carry0)
        for ls, (cr, ci) in zip(tiles, carry):
            cre_ref[:, ls] = cr[0:1, :]
            cim_ref[:, ls] = ci[0:1, :]

    ys = []
    for half in range(2):
        cols, lns = slice(half * hw, (half + 1) * hw), slice(half * hn, (half + 1) * hn)
        ys.append(jnp.dot(hre_ref[:, lns].astype(BF16), wcre_ref[lns, cols], preferred_element_type=F32)
                  - jnp.dot(him_ref[:, lns].astype(BF16), wcim_ref[lns, cols], preferred_element_type=F32))
    y = jnp.concatenate(ys, axis=1)
    y = y + d_ref[...] * u
    y = jax.nn.gelu(y)
    y = y * jax.nn.sigmoid(_bdot(y, wg_ref[...]) + bg_ref[...])
    o_ref[0] = y * gb_ref[0]


def _ssm_branch(u, gb, wbre, wbim, pre, pim, wcre, wcim, d_skip, w_glu, b_glu):
    bsz, seq, sw = u.shape
    tc = SSM_CHUNK
    n = SSM_LANES
    row = pl.BlockSpec((1, tc, sw), lambda b, i: (b, i, 0))
    full = lambda r, c: pl.BlockSpec((r, c), lambda b, i: (0, 0))
    return pl.pallas_call(
        _ssm_kernel,
        grid=(bsz, seq // tc),
        in_specs=[row, row, full(sw, n), full(sw, n), full(SUBLANES, n), full(SUBLANES, n),
                  full(n, sw), full(n, sw), full(1, sw), full(sw, sw), full(1, sw)],
        out_specs=row,
        out_shape=jax.ShapeDtypeStruct((bsz, seq, sw), F32),
        scratch_shapes=[pltpu.VMEM((tc, n), F32), pltpu.VMEM((tc, n), F32),
                        pltpu.VMEM((1, n), F32), pltpu.VMEM((1, n), F32)],
        compiler_params=pltpu.CompilerParams(
            dimension_semantics=("parallel", "arbitrary"), vmem_limit_bytes=VMEM_LIMIT_BYTES),
    )(u, gb, wbre, wbim, pre, pim, wcre, wcim, d_skip.reshape(1, sw), w_glu.astype(BF16),
      b_glu.reshape(1, sw))


def _mid_kernel(x_ref, ya_ref, yb_ref, mod0_ref, mod1_ref, g_ref, wo_ref, wi_ref,
                x1_ref, q_ref, k_ref, v_ref, sg_ref):
    cw, aw = CONV_WIDTH, ATTN_WIDTH
    y = (jnp.dot(ya_ref[0].astype(BF16), wo_ref[0:cw, :], preferred_element_type=F32)
         + jnp.dot(yb_ref[0].astype(BF16), wo_ref[cw:, :], preferred_element_type=F32))
    x1 = x_ref[0] + mod0_ref[0][2:3] * y
    x1_ref[0] = x1
    mod1 = mod1_ref[0]
    h = _rms_modulate(x1, g_ref[...], mod1[1:2], mod1[0:1]).astype(BF16)
    scale = ATTN_HEAD_DIM ** -0.5 * LOG2_E
    q_ref[0] = (jnp.dot(h, wi_ref[:, 0:aw], preferred_element_type=F32) * scale).astype(BF16)
    k_ref[0] = jnp.dot(h, wi_ref[:, aw:2 * aw], preferred_element_type=F32).astype(BF16)
    v_ref[0] = jnp.dot(h, wi_ref[:, 2 * aw:3 * aw], preferred_element_type=F32).astype(BF16)
    sg_ref[0] = _silu(jnp.dot(h, wi_ref[:, 3 * aw:4 * aw], preferred_element_type=F32))


def _mid(x, ya, yb, mod0, mod1, norm_g, w_out, w_in):
    bsz, seq, d = x.shape
    tl = ROW_TILE
    row = lambda width: pl.BlockSpec((1, tl, width), lambda b, i: (b, i, 0))
    modspec = pl.BlockSpec((1, 3, d), lambda b, i: (b, 0, 0))
    full = lambda r, c: pl.BlockSpec((r, c), lambda b, i: (0, 0))
    aw = ATTN_WIDTH
    return pl.pallas_call(
        _mid_kernel,
        grid=(bsz, seq // tl),
        in_specs=[row(d), row(CONV_WIDTH), row(SSM_WIDTH), modspec, modspec, full(1, d),
                  full(CONV_WIDTH + SSM_WIDTH, d), full(d, 4 * aw)],
        out_specs=[row(d), row(aw), row(aw), row(aw), row(aw)],
        out_shape=[jax.ShapeDtypeStruct((bsz, seq, d), F32),
                   jax.ShapeDtypeStruct((bsz, seq, aw), BF16),
                   jax.ShapeDtypeStruct((bsz, seq, aw), BF16),
                   jax.ShapeDtypeStruct((bsz, seq, aw), BF16),
                   jax.ShapeDtypeStruct((bsz, seq, aw), F32)],
        compiler_params=pltpu.CompilerParams(
            dimension_semantics=("parallel", "parallel"), vmem_limit_bytes=VMEM_LIMIT_BYTES),
    )(x, ya, yb, mod0, mod1, norm_g.reshape(1, d), w_out.astype(BF16), w_in.astype(BF16))


def _attn_kernel(q_ref, k_ref, v_ref, o_ref, acc_ref, carry_ref, qn_ref, sp0_ref, sp1_ref, e0_ref, e1_ref):
    blk = ATTN_BLOCK
    pair = 2 * ATTN_HEAD_DIM
    heads = range(2 * ATTN_PAIRS)
    cols = [slice((s // 2) * pair, (s // 2 + 1) * pair) for s in heads]
    qi = pl.program_id(2)
    lane = lax.broadcasted_iota(jnp.int32, (blk, pair), 1)
    first_head = lane < ATTN_HEAD_DIM
    r_idx = lax.broadcasted_iota(jnp.int32, (blk, blk), 0)
    c_idx = lax.broadcasted_iota(jnp.int32, (blk, blk), 1)
    not_before = (r_idx >= c_idx).astype(BF16)
    suffix2 = jnp.concatenate([not_before, not_before], axis=0)
    causal = c_idx < r_idx

    for s in heads:
        q = q_ref[0, :, cols[s]].astype(F32)
        qn_ref[s] = jnp.where(first_head == (s % 2 == 0), -q, 0.0).astype(BF16)
    acc_ref[...] = jnp.zeros_like(acc_ref)
    carry_ref[...] = jnp.zeros_like(carry_ref)

    def a_matmul(start):
        return [lax.dot_general(qn_ref[s], k_ref[0, pl.ds(start, blk), cols[s]],
                                (((1,), (1,)), ((), ())), preferred_element_type=F32) for s in heads]

    def a_rest(zns, diagonal, sp_ref, e_ref):
        for s in heads:
            zn = zns[s]
            l = jnp.minimum(zn, 0.0) - jnp.log(1.0 + jnp.exp2(-jnp.abs(zn))) * LOG2_E
            if diagonal:
                l = jnp.where(causal, l, 0.0)
                zn = jnp.where(causal, zn, MASKED_LOGIT)
            hi = l.astype(BF16)
            lo = (l - hi.astype(F32)).astype(BF16)
            sp_ref[s] = jnp.concatenate([hi, lo], axis=1)
            e_ref[s] = zn

    def b_matmul(sp_ref):
        return [jnp.dot(sp_ref[s], suffix2, preferred_element_type=F32) for s in heads]

    def b_rest(sums, start, e_ref):
        for s in heads:
            w = jnp.exp2((sums[s] + carry_ref[s]) - e_ref[s])
            acc_ref[s] += jnp.dot(w.astype(BF16), v_ref[0, pl.ds(start, blk), cols[s]],
                                  preferred_element_type=F32)
            carry_ref[s] += sums[s][:, 0:1]

    def step(a_start, src, dst):
        zns = a_matmul(a_start)
        sums = b_matmul(src[0])
        a_rest(zns, False, *dst)
        b_rest(sums, a_start + blk, src[1])

    buf0, buf1 = (sp0_ref, e0_ref), (sp1_ref, e1_ref)
    diag_start = pl.multiple_of(qi * blk, blk)
    odd = qi % 2 == 1

    @pl.when(odd)
    def _():
        a_rest(a_matmul(diag_start), True, *buf1)
        step(pl.multiple_of(diag_start - blk, blk), buf1, buf0)

    @pl.when(jnp.logical_not(odd))
    def _():
        a_rest(a_matmul(diag_start), True, *buf0)

    top = qi - qi % 2

    def pair_body(p, _):
        first = pl.multiple_of((top - 2 * p - 1) * blk, blk)
        step(first, buf0, buf1)
        step(pl.multiple_of(first - blk, blk), buf1, buf0)
        return 0

    lax.fori_loop(0, qi // 2, pair_body, 0)
    b_rest(b_matmul(sp0_ref), 0, e0_ref)
    for p in range(ATTN_PAIRS):
        o_ref[0, :, cols[2 * p]] = jnp.where(first_head, acc_ref[2 * p], acc_ref[2 * p + 1])


def _attention(q, k, v):
    bsz, seq, aw = q.shape
    pair = 2 * ATTN_HEAD_DIM
    nh = 2 * ATTN_PAIRS
    width = ATTN_PAIRS * pair
    blk = ATTN_BLOCK
    return pl.pallas_call(
        _attn_kernel,
        grid=(bsz, aw // width, seq // blk),
        in_specs=[pl.BlockSpec((1, blk, width), lambda b, h, i: (b, i, h)),
                  pl.BlockSpec((1, seq, width), lambda b, h, i: (b, 0, h)),
                  pl.BlockSpec((1, seq, width), lambda b, h, i: (b, 0, h))],
        out_specs=pl.BlockSpec((1, blk, width), lambda b, h, i: (b, i, h)),
        out_shape=jax.ShapeDtypeStruct((bsz, seq, aw), F32),
        scratch_shapes=[pltpu.VMEM((nh, blk, pair), F32),
                        pltpu.VMEM((nh, blk, 1), F32),
                        pltpu.VMEM((nh, blk, pair), BF16),
                        pltpu.VMEM((nh, blk, 2 * blk), BF16), pltpu.VMEM((nh, blk, 2 * blk), BF16),
                        pltpu.VMEM((nh, blk, blk), F32), pltpu.VMEM((nh, blk, blk), F32)],
        compiler_params=pltpu.CompilerParams(
            dimension_semantics=("parallel", "parallel", "parallel")),
    )(q, k, v)


def _final_kernel(x_ref, o_ref, sg_ref, mod_ref, wo_ref, g_ref, out_ref):
    y = _bdot(o_ref[0] * sg_ref[0], wo_ref[...])
    x2 = x_ref[0] + mod_ref[0][2:3] * y
    ms = jnp.mean(x2 * x2, axis=-1, keepdims=True)
    out_ref[0] = x2 * lax.rsqrt(ms + EPS) * g_ref[...]


def _final(x1, o, sg, mod1, w_out, final_g):
    bsz, seq, d = x1.shape
    tl = ROW_TILE
    row = pl.BlockSpec((1, tl, d), lambda b, i: (b, i, 0))
    return pl.pallas_call(
        _final_kernel,
        grid=(bsz, seq // tl),
        in_specs=[row, row, row,
                  pl.BlockSpec((1, 3, d), lambda b, i: (b, 0, 0)),
                  pl.BlockSpec((ATTN_WIDTH, d), lambda b, i: (0, 0)),
                  pl.BlockSpec((1, d), lambda b, i: (0, 0))],
        out_specs=row,
        out_shape=jax.ShapeDtypeStruct((bsz, seq, d), F32),
        compiler_params=pltpu.CompilerParams(dimension_semantics=("parallel", "parallel")),
    )(x1, o, sg, mod1, w_out.astype(BF16), final_g.reshape(1, d))


def kernel(x, c, l0_norm_g, l0_w_ada, l0_b_ada, l0_w_in, l0_conv_w, l0_conv_b, l0_conv_ln_g, l0_conv_ln_b, l0_ssm_lam_re, l0_ssm_lam_im, l0_ssm_log_dt, l0_ssm_b_re, l0_ssm_b_im, l0_ssm_c_re, l0_ssm_c_im, l0_ssm_d, l0_ssm_w_glu, l0_ssm_b_glu, l0_w_out, l1_norm_g, l1_w_ada, l1_b_ada, l1_w_in, l1_w_out, final_norm_g):
    bsz, seq, d = x.shape
    mod0 = _ada(c, l0_w_ada, l0_b_ada).reshape(bsz, 3, d)
    mod1 = _ada(c, l1_w_ada, l1_b_ada).reshape(bsz, 3, d)

    hg, ga, u, gb = _even_in(x, mod0, l0_norm_g, l0_w_in)
    ya = _conv_branch(hg, ga, l0_conv_w, l0_conv_b, l0_conv_ln_g, l0_conv_ln_b)

    pre, pim, bbre, bbim = _ssm_prep(l0_ssm_lam_re, l0_ssm_lam_im, l0_ssm_log_dt, l0_ssm_b_re, l0_ssm_b_im)
    yb = _ssm_branch(u, gb, _block_diag_in(bbre), _block_diag_in(bbim), pre, pim,
                     _block_diag_out(l0_ssm_c_re), _block_diag_out(l0_ssm_c_im),
                     l0_ssm_d, l0_ssm_w_glu, l0_ssm_b_glu)

    x1, q, k, v, sg = _mid(x, ya, yb, mod0, mod1, l1_norm_g, l0_w_out, l1_w_in)
    o = _attention(q, k, v)
    return _final(x1, o, sg, mod1, l1_w_out, final_norm_g)
```

```python
import math

import jax
import jax.numpy as jnp
from jax import lax
from jax.experimental import pallas as pl
from jax.experimental.pallas import tpu as pltpu

F32 = jnp.float32
BF16 = jnp.bfloat16

D_MODEL = 1024
CONV_WIDTH = 1024
CONV_KERNEL = 31
SSM_WIDTH = 512
SSM_GROUP = 16
SSM_GROUPS = SSM_WIDTH // SSM_GROUP
SSM_STATE = 64
SSM_LANES = SSM_GROUPS * SSM_STATE
ATTN_HEADS = 16
ATTN_HEAD_DIM = 64
ATTN_WIDTH = ATTN_HEADS * ATTN_HEAD_DIM
EPS = 1e-6
LOG2_E = math.log2(math.e)
MASKED_LOGIT = 1e30

SUBLANES = 8
LANES = 128
VMEM_LIMIT_BYTES = 56 * 1024 * 1024

ROW_TILE = 256
CONV_TILE = 128
CONV_HALO = 32
CONV_CHUNK = 32
SSM_CHUNK = 32
SSM_SCAN_TILES = 4
SSM_SCAN_UNROLL = 4
ATTN_BLOCK = 256
ATTN_PAIRS = 2


def _silu(x):
    return x * jax.nn.sigmoid(x)


def _bdot(a, b):
    return jnp.dot(a.astype(BF16), b.astype(BF16), preferred_element_type=F32)


def _rms_modulate(x, g, scale, shift):
    ms = jnp.mean(x * x, axis=-1, keepdims=True)
    return (x * lax.rsqrt(ms + EPS) * g) * (1.0 + scale) + shift


def _ada_kernel(c_ref, w_ref, b_ref, o_ref):
    o_ref[...] = _bdot(_silu(c_ref[...]), w_ref[...]) + b_ref[...]


def _ada(c, w_ada, b_ada):
    bsz, d = c.shape
    n = w_ada.shape[1]
    tn = d
    return pl.pallas_call(
        _ada_kernel,
        grid=(n // tn,),
        in_specs=[pl.BlockSpec((bsz, d), lambda j: (0, 0)),
                  pl.BlockSpec((d, tn), lambda j: (0, j)),
                  pl.BlockSpec((1, tn), lambda j: (0, j))],
        out_specs=pl.BlockSpec((bsz, tn), lambda j: (0, j)),
        out_shape=jax.ShapeDtypeStruct((bsz, n), F32),
        compiler_params=pltpu.CompilerParams(dimension_semantics=("parallel",)),
    )(c, w_ada, b_ada.reshape(1, n))


def _even_in_kernel(x_ref, mod_ref, g_ref, w_ref, hg_ref, ga_ref, u_ref, gb_ref):
    mod = mod_ref[0]
    h = _rms_modulate(x_ref[0], g_ref[...], mod[1:2], mod[0:1]).astype(BF16)
    cw, sw = CONV_WIDTH, SSM_WIDTH
    val = jnp.dot(h, w_ref[:, 0:cw], preferred_element_type=F32)
    glu = jnp.dot(h, w_ref[:, cw:2 * cw], preferred_element_type=F32)
    hg_ref[0] = val * jax.nn.sigmoid(glu)
    ga_ref[0] = _silu(jnp.dot(h, w_ref[:, 2 * cw:3 * cw], preferred_element_type=F32))
    u_ref[...] = jnp.dot(h, w_ref[:, 3 * cw:3 * cw + sw], preferred_element_type=F32)
    gb_ref[...] = _silu(jnp.dot(h, w_ref[:, 3 * cw + sw:3 * cw + 2 * sw], preferred_element_type=F32))


def _time_major_spec(tl):
    return pl.BlockSpec((tl, SSM_WIDTH), lambda b, i: (i, b))


def _even_in(x, mod, norm_g, w_in):
    bsz, seq, d = x.shape
    tl = ROW_TILE
    n_in = w_in.shape[1]
    row = lambda width: pl.BlockSpec((1, tl, width), lambda b, i: (b, i, 0))
    return pl.pallas_call(
        _even_in_kernel,
        grid=(bsz, seq // tl),
        in_specs=[row(d),
                  pl.BlockSpec((1, 3, d), lambda b, i: (b, 0, 0)),
                  pl.BlockSpec((1, d), lambda b, i: (0, 0)),
                  pl.BlockSpec((d, n_in), lambda b, i: (0, 0))],
        out_specs=[row(CONV_WIDTH), row(CONV_WIDTH), _time_major_spec(tl), _time_major_spec(tl)],
        out_shape=[jax.ShapeDtypeStruct((bsz, seq, CONV_WIDTH), F32),
                   jax.ShapeDtypeStruct((bsz, seq, CONV_WIDTH), F32),
                   jax.ShapeDtypeStruct((seq, bsz * SSM_WIDTH), F32),
                   jax.ShapeDtypeStruct((seq, bsz * SSM_WIDTH), F32)],
        compiler_params=pltpu.CompilerParams(
            dimension_semantics=("parallel", "parallel"), vmem_limit_bytes=VMEM_LIMIT_BYTES),
    )(x, mod, norm_g.reshape(1, d), w_in.astype(BF16))


def _conv_kernel(cur_ref, halo_ref, ga_ref, w_ref, cb_ref, lg_ref, lb_ref, o_ref, buf_ref, sh_ref):
    tl = cur_ref.shape[1]
    first = pl.program_id(1) == 0
    buf_ref[0:CONV_HALO, :] = jnp.where(first, 0.0, halo_ref[0])
    buf_ref[CONV_HALO:CONV_HALO + tl, :] = cur_ref[0]
    lead = CONV_HALO - (CONV_KERNEL - 1)
    rows = sh_ref.shape[1]
    for r in range(1, SUBLANES):
        sh_ref[r - 1] = buf_ref[r:r + rows, :]

    groups = CONV_CHUNK // SUBLANES
    for r0 in range(0, tl, CONV_CHUNK):
        acc = jnp.zeros((groups, SUBLANES, CONV_WIDTH), F32)
        for k in range(CONV_KERNEL):
            r, base = (lead + k) % SUBLANES, r0 + (lead + k) // SUBLANES * SUBLANES
            src = buf_ref if r == 0 else sh_ref.at[r - 1]
            rows8 = src[base:base + CONV_CHUNK, :].reshape(groups, SUBLANES, CONV_WIDTH)
            acc = acc + rows8 * w_ref[k * SUBLANES:(k + 1) * SUBLANES, :][None]
        acc = acc.reshape(CONV_CHUNK, CONV_WIDTH) + cb_ref[...]
        mu = jnp.mean(acc, axis=-1, keepdims=True)
        cen = acc - mu
        var = jnp.mean(cen * cen, axis=-1, keepdims=True)
        y = cen * lax.rsqrt(var + EPS) * lg_ref[...] + lb_ref[...]
        o_ref[0, r0:r0 + CONV_CHUNK, :] = _silu(y) * ga_ref[0, r0:r0 + CONV_CHUNK, :]


def _conv_branch(hg, ga, conv_w, conv_b, ln_g, ln_b):
    bsz, seq, cw = hg.shape
    tl = CONV_TILE
    halo_per_tile = tl // CONV_HALO
    row = pl.BlockSpec((1, tl, cw), lambda b, i: (b, i, 0))
    vec = pl.BlockSpec((1, cw), lambda b, i: (0, 0))
    return pl.pallas_call(
        _conv_kernel,
        grid=(bsz, seq // tl),
        in_specs=[row,
                  pl.BlockSpec((1, CONV_HALO, cw),
                               lambda b, i: (b, jnp.maximum(i * halo_per_tile - 1, 0), 0)),
                  row,
                  pl.BlockSpec((CONV_KERNEL * SUBLANES, cw), lambda b, i: (0, 0)),
                  vec, vec, vec],
        out_specs=row,
        out_shape=jax.ShapeDtypeStruct((bsz, seq, cw), F32),
        scratch_shapes=[pltpu.VMEM((CONV_HALO + tl, cw), F32),
                        pltpu.VMEM((SUBLANES - 1, CONV_HALO + tl - SUBLANES, cw), F32)],
        compiler_params=pltpu.CompilerParams(dimension_semantics=("parallel", "parallel")),
    )(hg, hg, ga, jnp.repeat(conv_w, SUBLANES, axis=0), conv_b.reshape(1, cw), ln_g.reshape(1, cw),
      ln_b.reshape(1, cw))


def _ssm_prep_kernel(lre_ref, lim_ref, ldt_ref, bre_ref, bim_ref, pre_ref, pim_ref, bbre_ref, bbim_ref):
    lre, lim = lre_ref[...], lim_ref[...]
    dt = jnp.exp(ldt_ref[...])
    mag = jnp.exp(lre * dt)
    are, aim = mag * jnp.cos(lim * dt), mag * jnp.sin(lim * dt)
    nre, nim = are - 1.0, aim
    den = lre * lre + lim * lim
    cre = (nre * lre + nim * lim) / den
    cim = (nim * lre - nre * lim) / den
    bre, bim = bre_ref[...], bim_ref[...]
    bbre_ref[...] = cre * bre - cim * bim
    bbim_ref[...] = cre * bim + cim * bre
    pre_ref[...] = are
    pim_ref[...] = aim


def _ssm_prep(lam_re, lam_im, log_dt, b_re, b_im):
    n = SSM_LANES
    flat = lambda t: t.reshape(1, n)
    to_lanes = lambda t: t.reshape(n, SSM_GROUP).T
    ldt = jnp.broadcast_to(log_dt[:, None], (SSM_GROUPS, SSM_STATE))
    outs = pl.pallas_call(
        _ssm_prep_kernel,
        out_shape=[jax.ShapeDtypeStruct((1, n), F32), jax.ShapeDtypeStruct((1, n), F32),
                   jax.ShapeDtypeStruct((SSM_GROUP, n), F32), jax.ShapeDtypeStruct((SSM_GROUP, n), F32)],
    )(flat(lam_re), flat(lam_im), flat(ldt), to_lanes(b_re), to_lanes(b_im))
    return outs


def _group_mask_in():
    r = jnp.arange(SSM_WIDTH)[:, None] // SSM_GROUP
    c = jnp.arange(SSM_LANES)[None, :] // SSM_STATE
    return r == c


def _block_diag_in(bb):
    return jnp.where(_group_mask_in(), jnp.tile(bb, (SSM_GROUPS, 1)), 0.0).astype(BF16)


def _block_diag_out(cmat):
    per_lane = cmat.transpose(0, 2, 1).reshape(SSM_LANES, SSM_GROUP)
    return jnp.where(_group_mask_in().T, jnp.tile(per_lane, (1, SSM_GROUPS)), 0.0).astype(BF16)


def _ssm_kernel(u_ref, gb_ref, wbre_ref, wbim_ref, pre_ref, pim_ref, wcre_ref, wcim_ref,
                d_ref, wg_ref, bg_ref, o_ref, hre_ref, him_ref, sre_ref, sim_ref):
    rows = u_ref.shape[0]
    bsz = sre_ref.shape[0]

    @pl.when(pl.program_id(0) == 0)
    def _():
        sre_ref[...] = jnp.zeros_like(sre_ref)
        sim_ref[...] = jnp.zeros_like(sim_ref)

    u = u_ref[...]
    ub = u.astype(BF16)
    hw, hn = SSM_WIDTH // 2, SSM_LANES // 2
    for half in range(2):
        chans, lns = slice(half * hw, (half + 1) * hw), slice(half * hn, (half + 1) * hn)
        hre_ref[:, lns] = jnp.dot(ub[:, chans], wbre_ref[chans, lns], preferred_element_type=F32)
        him_ref[:, lns] = jnp.dot(ub[:, chans], wbim_ref[chans, lns], preferred_element_type=F32)

    width = SSM_SCAN_TILES * LANES
    for j0 in range(0, SSM_LANES, width):
        ls = slice(j0, j0 + width)
        are = jnp.broadcast_to(pre_ref[:, ls], (bsz, width))
        aim = jnp.broadcast_to(pim_ref[:, ls], (bsz, width))

        def tick(t, h):
            hr, hi = h
            r0 = pl.multiple_of(t * bsz, bsz)
            nr = (are * hr - aim * hi) + hre_ref[pl.ds(r0, bsz), ls]
            ni = (are * hi + aim * hr) + him_ref[pl.ds(r0, bsz), ls]
            hre_ref[pl.ds(r0, bsz), ls] = nr
            him_ref[pl.ds(r0, bsz), ls] = ni
            return nr, ni

        hr, hi = lax.fori_loop(0, rows // bsz, tick, (sre_ref[:, ls], sim_ref[:, ls]),
                               unroll=SSM_SCAN_UNROLL)
        sre_ref[:, ls] = hr
        sim_ref[:, ls] = hi

    ys = []
    for half in range(2):
        cols, lns = slice(half * hw, (half + 1) * hw), slice(half * hn, (half + 1) * hn)
        ys.append(jnp.dot(hre_ref[:, lns].astype(BF16), wcre_ref[lns, cols], preferred_element_type=F32)
                  - jnp.dot(him_ref[:, lns].astype(BF16), wcim_ref[lns, cols], preferred_element_type=F32))
    y = jnp.concatenate(ys, axis=1)
    y = y + d_ref[...] * u
    y = jax.nn.gelu(y)
    y = y * jax.nn.sigmoid(_bdot(y, wg_ref[...]) + bg_ref[...])
    o_ref[...] = y * gb_ref[...]


def _ssm_branch(u, gb, bsz, wbre, wbim, pre, pim, wcre, wcim, d_skip, w_glu, b_glu):
    total, sw = u.shape
    assert bsz == SUBLANES, "the scan keeps the batch on the sublanes of one vreg tile"
    rows = SSM_CHUNK * bsz
    n = SSM_LANES
    row = pl.BlockSpec((rows, sw), lambda i: (i, 0))
    full = lambda r, c: pl.BlockSpec((r, c), lambda i: (0, 0))
    return pl.pallas_call(
        _ssm_kernel,
        grid=(total // rows,),
        in_specs=[row, row, full(sw, n), full(sw, n), full(1, n), full(1, n),
                  full(n, sw), full(n, sw), full(1, sw), full(sw, sw), full(1, sw)],
        out_specs=row,
        out_shape=jax.ShapeDtypeStruct((total, sw), F32),
        scratch_shapes=[pltpu.VMEM((rows, n), F32), pltpu.VMEM((rows, n), F32),
                        pltpu.VMEM((bsz, n), F32), pltpu.VMEM((bsz, n), F32)],
        compiler_params=pltpu.CompilerParams(
            dimension_semantics=("arbitrary",), vmem_limit_bytes=VMEM_LIMIT_BYTES),
    )(u, gb, wbre, wbim, pre, pim, wcre, wcim, d_skip.reshape(1, sw), w_glu.astype(BF16),
      b_glu.reshape(1, sw))


def _mid_kernel(x_ref, ya_ref, yb_ref, mod0_ref, mod1_ref, g_ref, wo_ref, wi_ref,
                x1_ref, q_ref, k_ref, v_ref, sg_ref):
    cw, aw = CONV_WIDTH, ATTN_WIDTH
    y = (jnp.dot(ya_ref[0].astype(BF16), wo_ref[0:cw, :], preferred_element_type=F32)
         + jnp.dot(yb_ref[...].astype(BF16), wo_ref[cw:, :], preferred_element_type=F32))
    x1 = x_ref[0] + mod0_ref[0][2:3] * y
    x1_ref[0] = x1
    mod1 = mod1_ref[0]
    h = _rms_modulate(x1, g_ref[...], mod1[1:2], mod1[0:1]).astype(BF16)
    scale = ATTN_HEAD_DIM ** -0.5 * LOG2_E
    q_ref[0] = (jnp.dot(h, wi_ref[:, 0:aw], preferred_element_type=F32) * scale).astype(BF16)
    k_ref[0] = jnp.dot(h, wi_ref[:, aw:2 * aw], preferred_element_type=F32).astype(BF16)
    v_ref[0] = jnp.dot(h, wi_ref[:, 2 * aw:3 * aw], preferred_element_type=F32).astype(BF16)
    sg_ref[0] = _silu(jnp.dot(h, wi_ref[:, 3 * aw:4 * aw], preferred_element_type=F32))


def _mid(x, ya, yb, mod0, mod1, norm_g, w_out, w_in):
    bsz, seq, d = x.shape
    tl = ROW_TILE
    row = lambda width: pl.BlockSpec((1, tl, width), lambda b, i: (b, i, 0))
    modspec = pl.BlockSpec((1, 3, d), lambda b, i: (b, 0, 0))
    full = lambda r, c: pl.BlockSpec((r, c), lambda b, i: (0, 0))
    aw = ATTN_WIDTH
    return pl.pallas_call(
        _mid_kernel,
        grid=(bsz, seq // tl),
        in_specs=[row(d), row(CONV_WIDTH), _time_major_spec(tl), modspec, modspec, full(1, d),
                  full(CONV_WIDTH + SSM_WIDTH, d), full(d, 4 * aw)],
        out_specs=[row(d), row(aw), row(aw), row(aw), row(aw)],
        out_shape=[jax.ShapeDtypeStruct((bsz, seq, d), F32),
                   jax.ShapeDtypeStruct((bsz, seq, aw), BF16),
                   jax.ShapeDtypeStruct((bsz, seq, aw), BF16),
                   jax.ShapeDtypeStruct((bsz, seq, aw), BF16),
                   jax.ShapeDtypeStruct((bsz, seq, aw), F32)],
        compiler_params=pltpu.CompilerParams(
            dimension_semantics=("parallel", "parallel"), vmem_limit_bytes=VMEM_LIMIT_BYTES),
    )(x, ya, yb, mod0, mod1, norm_g.reshape(1, d), w_out.astype(BF16), w_in.astype(BF16))


def _attn_kernel(q_ref, k_ref, v_ref, o_ref, acc_ref, carry_ref, qn_ref, sp0_ref, sp1_ref, e0_ref, e1_ref):
    blk = ATTN_BLOCK
    pair = 2 * ATTN_HEAD_DIM
    heads = range(2 * ATTN_PAIRS)
    cols = [slice((s // 2) * pair, (s // 2 + 1) * pair) for s in heads]
    qi = pl.program_id(2)
    lane = lax.broadcasted_iota(jnp.int32, (blk, pair), 1)
    first_head = lane < ATTN_HEAD_DIM
    r_idx = lax.broadcasted_iota(jnp.int32, (blk, blk), 0)
    c_idx = lax.broadcasted_iota(jnp.int32, (blk, blk), 1)
    not_before = (r_idx >= c_idx).astype(BF16)
    causal = c_idx < r_idx

    for s in heads:
        q = q_ref[0, :, cols[s]].astype(F32)
        qn_ref[s] = jnp.where(first_head == (s % 2 == 0), -q, 0.0).astype(BF16)
    acc_ref[...] = jnp.zeros_like(acc_ref)
    carry_ref[...] = jnp.zeros_like(carry_ref)

    def a_matmul(start):
        return [lax.dot_general(qn_ref[s], k_ref[0, pl.ds(start, blk), cols[s]],
                                (((1,), (1,)), ((), ())), preferred_element_type=F32) for s in heads]

    def a_rest(zns, diagonal, sp_ref, e_ref):
        for s in heads:
            zn = zns[s]
            l = jnp.minimum(zn, 0.0) - jnp.log(1.0 + jnp.exp2(-jnp.abs(zn))) * LOG2_E
            if diagonal:
                l = jnp.where(causal, l, 0.0)
                zn = jnp.where(causal, zn, MASKED_LOGIT)
            sp_ref[s] = l.astype(BF16)
            e_ref[s] = zn

    def b_matmul(sp_ref):
        return [jnp.dot(sp_ref[s], not_before, preferred_element_type=F32) for s in heads]

    def b_rest(sums, start, e_ref):
        for s in heads:
            w = jnp.exp2((sums[s] + carry_ref[s]) - e_ref[s])
            acc_ref[s] += jnp.dot(w.astype(BF16), v_ref[0, pl.ds(start, blk), cols[s]],
                                  preferred_element_type=F32)
            carry_ref[s] += sums[s][:, 0:1]

    def step(a_start, src, dst):
        zns = a_matmul(a_start)
        sums = b_matmul(src[0])
        a_rest(zns, False, *dst)
        b_rest(sums, a_start + blk, src[1])

    buf0, buf1 = (sp0_ref, e0_ref), (sp1_ref, e1_ref)
    diag_start = pl.multiple_of(qi * blk, blk)
    odd = qi % 2 == 1

    @pl.when(odd)
    def _():
        a_rest(a_matmul(diag_start), True, *buf1)
        step(pl.multiple_of(diag_start - blk, blk), buf1, buf0)

    @pl.when(jnp.logical_not(odd))
    def _():
        a_rest(a_matmul(diag_start), True, *buf0)

    top = qi - qi % 2

    def pair_body(p, _):
        first = pl.multiple_of((top - 2 * p - 1) * blk, blk)
        step(first, buf0, buf1)
        step(pl.multiple_of(first - blk, blk), buf1, buf0)
        return 0

    lax.fori_loop(0, qi // 2, pair_body, 0)
    b_rest(b_matmul(sp0_ref), 0, e0_ref)
    for p in range(ATTN_PAIRS):
        o_ref[0, :, cols[2 * p]] = jnp.where(first_head, acc_ref[2 * p], acc_ref[2 * p + 1])


def _attention(q, k, v):
    bsz, seq, aw = q.shape
    pair = 2 * ATTN_HEAD_DIM
    nh = 2 * ATTN_PAIRS
    width = ATTN_PAIRS * pair
    blk = ATTN_BLOCK
    return pl.pallas_call(
        _attn_kernel,
        grid=(bsz, aw // width, seq // blk),
        in_specs=[pl.BlockSpec((1, blk, width), lambda b, h, i: (b, i, h)),
                  pl.BlockSpec((1, seq, width), lambda b, h, i: (b, 0, h)),
                  pl.BlockSpec((1, seq, width), lambda b, h, i: (b, 0, h))],
        out_specs=pl.BlockSpec((1, blk, width), lambda b, h, i: (b, i, h)),
        out_shape=jax.ShapeDtypeStruct((bsz, seq, aw), F32),
        scratch_shapes=[pltpu.VMEM((nh, blk, pair), F32),
                        pltpu.VMEM((nh, blk, 1), F32),
                        pltpu.VMEM((nh, blk, pair), BF16),
                        pltpu.VMEM((nh, blk, blk), BF16), pltpu.VMEM((nh, blk, blk), BF16),
                        pltpu.VMEM((nh, blk, blk), F32), pltpu.VMEM((nh, blk, blk), F32)],
        compiler_params=pltpu.CompilerParams(
            dimension_semantics=("parallel", "parallel", "parallel")),
    )(q, k, v)


def _final_kernel(x_ref, o_ref, sg_ref, mod_ref, wo_ref, g_ref, out_ref):
    y = _bdot(o_ref[0] * sg_ref[0], wo_ref[...])
    x2 = x_ref[0] + mod_ref[0][2:3] * y
    ms = jnp.mean(x2 * x2, axis=-1, keepdims=True)
    out_ref[0] = x2 * lax.rsqrt(ms + EPS) * g_ref[...]


def _final(x1, o, sg, mod1, w_out, final_g):
    bsz, seq, d = x1.shape
    tl = ROW_TILE
    row = pl.BlockSpec((1, tl, d), lambda b, i: (b, i, 0))
    return pl.pallas_call(
        _final_kernel,
        grid=(bsz, seq // tl),
        in_specs=[row, row, row,
                  pl.BlockSpec((1, 3, d), lambda b, i: (b, 0, 0)),
                  pl.BlockSpec((ATTN_WIDTH, d), lambda b, i: (0, 0)),
                  pl.BlockSpec((1, d), lambda b, i: (0, 0))],
        out_specs=row,
        out_shape=jax.ShapeDtypeStruct((bsz, seq, d), F32),
        compiler_params=pltpu.CompilerParams(dimension_semantics=("parallel", "parallel")),
    )(x1, o, sg, mod1, w_out.astype(BF16), final_g.reshape(1, d))


def kernel(x, c, l0_norm_g, l0_w_ada, l0_b_ada, l0_w_in, l0_conv_w, l0_conv_b, l0_conv_ln_g, l0_conv_ln_b, l0_ssm_lam_re, l0_ssm_lam_im, l0_ssm_log_dt, l0_ssm_b_re, l0_ssm_b_im, l0_ssm_c_re, l0_ssm_c_im, l0_ssm_d, l0_ssm_w_glu, l0_ssm_b_glu, l0_w_out, l1_norm_g, l1_w_ada, l1_b_ada, l1_w_in, l1_w_out, final_norm_g):
    bsz, seq, d = x.shape
    mod0 = _ada(c, l0_w_ada, l0_b_ada).reshape(bsz, 3, d)
    mod1 = _ada(c, l1_w_ada, l1_b_ada).reshape(bsz, 3, d)

    hg, ga, u, gb = _even_in(x, mod0, l0_norm_g, l0_w_in)
    ya = _conv_branch(hg, ga, l0_conv_w, l0_conv_b, l0_conv_ln_g, l0_conv_ln_b)

    pre, pim, bbre, bbim = _ssm_prep(l0_ssm_lam_re, l0_ssm_lam_im, l0_ssm_log_dt, l0_ssm_b_re, l0_ssm_b_im)
    time_major = lambda t: t.reshape(seq * bsz, SSM_WIDTH)
    yb = _ssm_branch(time_major(u), time_major(gb), bsz, _block_diag_in(bbre), _block_diag_in(bbim), pre, pim,
                     _block_diag_out(l0_ssm_c_re), _block_diag_out(l0_ssm_c_im),
                     l0_ssm_d, l0_ssm_w_glu, l0_ssm_b_glu).reshape(seq, bsz * SSM_WIDTH)

    x1, q, k, v, sg = _mid(x, ya, yb, mod0, mod1, l1_norm_g, l0_w_out, l1_w_in)
    o = _attention(q, k, v)
    return _final(x1, o, sg, mod1, l1_w_out, final_norm_g)
```

```python
import math

import jax
import jax.numpy as jnp
from jax import lax
from jax.experimental import pallas as pl
from jax.experimental.pallas import tpu as pltpu

F32 = jnp.float32
BF16 = jnp.bfloat16

D_MODEL = 1024
CONV_WIDTH = 1024
CONV_KERNEL = 31
SSM_WIDTH = 512
SSM_GROUP = 16
SSM_GROUPS = SSM_WIDTH // SSM_GROUP
SSM_STATE = 64
SSM_LANES = SSM_GROUPS * SSM_STATE
ATTN_HEADS = 16
ATTN_HEAD_DIM = 64
ATTN_WIDTH = ATTN_HEADS * ATTN_HEAD_DIM
EPS = 1e-6
LOG2_E = math.log2(math.e)
MASKED_LOGIT = 1e30

SUBLANES = 8
LANES = 128
VMEM_LIMIT_BYTES = 56 * 1024 * 1024

ROW_TILE = 256
CONV_TILE = 128
CONV_HALO = 32
CONV_CHUNK = 32
SSM_CHUNK = 32
SSM_SCAN_TILES = 4
SSM_SCAN_UNROLL = 4
ATTN_BLOCK = 256
ATTN_PAIRS = 2


def _silu(x):
    return x * jax.nn.sigmoid(x)


def _bdot(a, b):
    return jnp.dot(a.astype(BF16), b.astype(BF16), preferred_element_type=F32)


def _rms_modulate(x, g, scale, shift):
    ms = jnp.mean(x * x, axis=-1, keepdims=True)
    return (x * lax.rsqrt(ms + EPS) * g) * (1.0 + scale) + shift


def _ada_kernel(c_ref, w_ref, b_ref, o_ref):
    o_ref[...] = _bdot(_silu(c_ref[...]), w_ref[...]) + b_ref[...]


def _ada(c, w_ada, b_ada):
    bsz, d = c.shape
    n = w_ada.shape[1]
    tn = d
    return pl.pallas_call(
        _ada_kernel,
        grid=(n // tn,),
        in_specs=[pl.BlockSpec((bsz, d), lambda j: (0, 0)),
                  pl.BlockSpec((d, tn), lambda j: (0, j)),
                  pl.BlockSpec((1, tn), lambda j: (0, j))],
        out_specs=pl.BlockSpec((bsz, tn), lambda j: (0, j)),
        out_shape=jax.ShapeDtypeStruct((bsz, n), F32),
        compiler_params=pltpu.CompilerParams(dimension_semantics=("parallel",)),
    )(c, w_ada, b_ada.reshape(1, n))


def _even_in_kernel(x_ref, mod_ref, g_ref, w_ref, hg_ref, ga_ref, u_ref, gb_ref):
    mod = mod_ref[0]
    h = _rms_modulate(x_ref[0], g_ref[...], mod[1:2], mod[0:1]).astype(BF16)
    cw, sw = CONV_WIDTH, SSM_WIDTH
    val = jnp.dot(h, w_ref[:, 0:cw], preferred_element_type=F32)
    glu = jnp.dot(h, w_ref[:, cw:2 * cw], preferred_element_type=F32)
    hg_ref[0] = val * jax.nn.sigmoid(glu)
    ga_ref[0] = _silu(jnp.dot(h, w_ref[:, 2 * cw:3 * cw], preferred_element_type=F32))
    u_ref[0] = jnp.dot(h, w_ref[:, 3 * cw:3 * cw + sw], preferred_element_type=F32)
    gb_ref[0] = _silu(jnp.dot(h, w_ref[:, 3 * cw + sw:3 * cw + 2 * sw], preferred_element_type=F32))


def _even_in(x, mod, norm_g, w_in):
    bsz, seq, d = x.shape
    tl = ROW_TILE
    n_in = w_in.shape[1]
    row = lambda width: pl.BlockSpec((1, tl, width), lambda b, i: (b, i, 0))
    return pl.pallas_call(
        _even_in_kernel,
        grid=(bsz, seq // tl),
        in_specs=[row(d),
                  pl.BlockSpec((1, 3, d), lambda b, i: (b, 0, 0)),
                  pl.BlockSpec((1, d), lambda b, i: (0, 0)),
                  pl.BlockSpec((d, n_in), lambda b, i: (0, 0))],
        out_specs=[row(CONV_WIDTH), row(CONV_WIDTH), row(SSM_WIDTH), row(SSM_WIDTH)],
        out_shape=[jax.ShapeDtypeStruct((bsz, seq, CONV_WIDTH), F32),
                   jax.ShapeDtypeStruct((bsz, seq, CONV_WIDTH), F32),
                   jax.ShapeDtypeStruct((bsz, seq, SSM_WIDTH), F32),
                   jax.ShapeDtypeStruct((bsz, seq, SSM_WIDTH), F32)],
        compiler_params=pltpu.CompilerParams(
            dimension_semantics=("parallel", "parallel"), vmem_limit_bytes=VMEM_LIMIT_BYTES),
    )(x, mod, norm_g.reshape(1, d), w_in.astype(BF16))


def _conv_kernel(cur_ref, halo_ref, ga_ref, w_ref, cb_ref, lg_ref, lb_ref, o_ref, buf_ref, sh_ref):
    tl = cur_ref.shape[1]
    first = pl.program_id(1) == 0
    buf_ref[0:CONV_HALO, :] = jnp.where(first, 0.0, halo_ref[0])
    buf_ref[CONV_HALO:CONV_HALO + tl, :] = cur_ref[0]
    lead = CONV_HALO - (CONV_KERNEL - 1)
    rows = sh_ref.shape[1]
    for r in range(1, SUBLANES):
        sh_ref[r - 1] = buf_ref[r:r + rows, :]

    groups = CONV_CHUNK // SUBLANES
    for r0 in range(0, tl, CONV_CHUNK):
        acc = jnp.zeros((groups, SUBLANES, CONV_WIDTH), F32)
        for k in range(CONV_KERNEL):
            r, base = (lead + k) % SUBLANES, r0 + (lead + k) // SUBLANES * SUBLANES
            src = buf_ref if r == 0 else sh_ref.at[r - 1]
            rows8 = src[base:base + CONV_CHUNK, :].reshape(groups, SUBLANES, CONV_WIDTH)
            acc = acc + rows8 * w_ref[k * SUBLANES:(k + 1) * SUBLANES, :][None]
        acc = acc.reshape(CONV_CHUNK, CONV_WIDTH) + cb_ref[...]
        mu = jnp.mean(acc, axis=-1, keepdims=True)
        cen = acc - mu
        var = jnp.mean(cen * cen, axis=-1, keepdims=True)
        y = cen * lax.rsqrt(var + EPS) * lg_ref[...] + lb_ref[...]
        o_ref[0, r0:r0 + CONV_CHUNK, :] = _silu(y) * ga_ref[0, r0:r0 + CONV_CHUNK, :]


def _conv_branch(hg, ga, conv_w, conv_b, ln_g, ln_b):
    bsz, seq, cw = hg.shape
    tl = CONV_TILE
    halo_per_tile = tl // CONV_HALO
    row = pl.BlockSpec((1, tl, cw), lambda b, i: (b, i, 0))
    vec = pl.BlockSpec((1, cw), lambda b, i: (0, 0))
    return pl.pallas_call(
        _conv_kernel,
        grid=(bsz, seq // tl),
        in_specs=[row,
                  pl.BlockSpec((1, CONV_HALO, cw),
                               lambda b, i: (b, jnp.maximum(i * halo_per_tile - 1, 0), 0)),
                  row,
                  pl.BlockSpec((CONV_KERNEL * SUBLANES, cw), lambda b, i: (0, 0)),
                  vec, vec, vec],
        out_specs=row,
        out_shape=jax.ShapeDtypeStruct((bsz, seq, cw), F32),
        scratch_shapes=[pltpu.VMEM((CONV_HALO + tl, cw), F32),
                        pltpu.VMEM((SUBLANES - 1, CONV_HALO + tl - SUBLANES, cw), F32)],
        compiler_params=pltpu.CompilerParams(dimension_semantics=("parallel", "parallel")),
    )(hg, hg, ga, jnp.repeat(conv_w, SUBLANES, axis=0), conv_b.reshape(1, cw), ln_g.reshape(1, cw),
      ln_b.reshape(1, cw))


def _ssm_prep_kernel(lre_ref, lim_ref, ldt_ref, bre_ref, bim_ref, pre_ref, pim_ref, bbre_ref, bbim_ref):
    lre, lim = lre_ref[...], lim_ref[...]
    dt = jnp.exp(ldt_ref[...])
    mag = jnp.exp(lre * dt)
    are, aim = mag * jnp.cos(lim * dt), mag * jnp.sin(lim * dt)
    nre, nim = are - 1.0, aim
    den = lre * lre + lim * lim
    cre = (nre * lre + nim * lim) / den
    cim = (nim * lre - nre * lim) / den
    bre, bim = bre_ref[...], bim_ref[...]
    bbre_ref[...] = cre * bre - cim * bim
    bbim_ref[...] = cre * bim + cim * bre
    pre_ref[...] = are
    pim_ref[...] = aim


def _ssm_prep(lam_re, lam_im, log_dt, b_re, b_im):
    n = SSM_LANES
    flat = lambda t: t.reshape(1, n)
    to_lanes = lambda t: t.reshape(n, SSM_GROUP).T
    ldt = jnp.broadcast_to(log_dt[:, None], (SSM_GROUPS, SSM_STATE))
    outs = pl.pallas_call(
        _ssm_prep_kernel,
        out_shape=[jax.ShapeDtypeStruct((1, n), F32), jax.ShapeDtypeStruct((1, n), F32),
                   jax.ShapeDtypeStruct((SSM_GROUP, n), F32), jax.ShapeDtypeStruct((SSM_GROUP, n), F32)],
    )(flat(lam_re), flat(lam_im), flat(ldt), to_lanes(b_re), to_lanes(b_im))
    return outs


def _group_mask_in():
    r = jnp.arange(SSM_WIDTH)[:, None] // SSM_GROUP
    c = jnp.arange(SSM_LANES)[None, :] // SSM_STATE
    return r == c


def _block_diag_in(bb):
    return jnp.where(_group_mask_in(), jnp.tile(bb, (SSM_GROUPS, 1)), 0.0).astype(BF16)


def _block_diag_out(cmat):
    per_lane = cmat.transpose(0, 2, 1).reshape(SSM_LANES, SSM_GROUP)
    return jnp.where(_group_mask_in().T, jnp.tile(per_lane, (1, SSM_GROUPS)), 0.0).astype(BF16)


def _permute_rows_f32(perm, x):
    hi = x.astype(BF16)
    rest = x - hi.astype(F32)
    mid = rest.astype(BF16)
    lo = (rest - mid.astype(F32)).astype(BF16)
    return (jnp.dot(perm, hi, preferred_element_type=F32) + jnp.dot(perm, mid, preferred_element_type=F32)
            + jnp.dot(perm, lo, preferred_element_type=F32))


def _ssm_kernel(u_ref, gb_ref, wbre_ref, wbim_ref, pre_ref, pim_ref, wcre_ref, wcim_ref,
                d_ref, wg_ref, bg_ref, o_ref, hre_ref, him_ref, sre_ref, sim_ref):
    bsz, tc, sw = u_ref.shape
    rows = bsz * tc

    @pl.when(pl.program_id(0) == 0)
    def _():
        sre_ref[...] = jnp.zeros_like(sre_ref)
        sim_ref[...] = jnp.zeros_like(sim_ref)

    u = u_ref[...].reshape(rows, sw)
    r_idx = lax.broadcasted_iota(jnp.int32, (rows, rows), 0)
    c_idx = lax.broadcasted_iota(jnp.int32, (rows, rows), 1)
    to_time_major = (c_idx == (r_idx % bsz) * tc + r_idx // bsz).astype(BF16)
    to_batch_major = (c_idx == (r_idx % tc) * bsz + r_idx // tc).astype(BF16)
    ub = jnp.dot(to_time_major, u.astype(BF16), preferred_element_type=F32).astype(BF16)
    hw, hn = SSM_WIDTH // 2, SSM_LANES // 2
    for half in range(2):
        chans, lns = slice(half * hw, (half + 1) * hw), slice(half * hn, (half + 1) * hn)
        hre_ref[:, lns] = jnp.dot(ub[:, chans], wbre_ref[chans, lns], preferred_element_type=F32)
        him_ref[:, lns] = jnp.dot(ub[:, chans], wbim_ref[chans, lns], preferred_element_type=F32)

    width = SSM_SCAN_TILES * LANES
    for j0 in range(0, SSM_LANES, width):
        ls = slice(j0, j0 + width)
        are = jnp.broadcast_to(pre_ref[:, ls], (bsz, width))
        aim = jnp.broadcast_to(pim_ref[:, ls], (bsz, width))

        def tick(t, h):
            hr, hi = h
            r0 = pl.multiple_of(t * bsz, bsz)
            nr = (are * hr - aim * hi) + hre_ref[pl.ds(r0, bsz), ls]
            ni = (are * hi + aim * hr) + him_ref[pl.ds(r0, bsz), ls]
            hre_ref[pl.ds(r0, bsz), ls] = nr
            him_ref[pl.ds(r0, bsz), ls] = ni
            return nr, ni

        hr, hi = lax.fori_loop(0, tc, tick, (sre_ref[:, ls], sim_ref[:, ls]), unroll=SSM_SCAN_UNROLL)
        sre_ref[:, ls] = hr
        sim_ref[:, ls] = hi

    ys = []
    for half in range(2):
        cols, lns = slice(half * hw, (half + 1) * hw), slice(half * hn, (half + 1) * hn)
        ys.append(jnp.dot(hre_ref[:, lns].astype(BF16), wcre_ref[lns, cols], preferred_element_type=F32)
                  - jnp.dot(him_ref[:, lns].astype(BF16), wcim_ref[lns, cols], preferred_element_type=F32))
    y = _permute_rows_f32(to_batch_major, jnp.concatenate(ys, axis=1))
    y = y + d_ref[...] * u
    y = jax.nn.gelu(y)
    y = y * jax.nn.sigmoid(_bdot(y, wg_ref[...]) + bg_ref[...])
    o_ref[...] = (y * gb_ref[...].reshape(rows, sw)).reshape(bsz, tc, sw)


def _ssm_branch(u, gb, wbre, wbim, pre, pim, wcre, wcim, d_skip, w_glu, b_glu):
    bsz, seq, sw = u.shape
    assert bsz == SUBLANES, "the scan keeps the batch on the sublanes of one vreg tile"
    tc = SSM_CHUNK
    n = SSM_LANES
    row = pl.BlockSpec((bsz, tc, sw), lambda i: (0, i, 0))
    full = lambda r, c: pl.BlockSpec((r, c), lambda i: (0, 0))
    return pl.pallas_call(
        _ssm_kernel,
        grid=(seq // tc,),
        in_specs=[row, row, full(sw, n), full(sw, n), full(1, n), full(1, n),
                  full(n, sw), full(n, sw), full(1, sw), full(sw, sw), full(1, sw)],
        out_specs=row,
        out_shape=jax.ShapeDtypeStruct((bsz, seq, sw), F32),
        scratch_shapes=[pltpu.VMEM((bsz * tc, n), F32), pltpu.VMEM((bsz * tc, n), F32),
                        pltpu.VMEM((bsz, n), F32), pltpu.VMEM((bsz, n), F32)],
        compiler_params=pltpu.CompilerParams(
            dimension_semantics=("arbitrary",), vmem_limit_bytes=VMEM_LIMIT_BYTES),
    )(u, gb, wbre, wbim, pre, pim, wcre, wcim, d_skip.reshape(1, sw), w_glu.astype(BF16),
      b_glu.reshape(1, sw))


def _mid_kernel(x_ref, ya_ref, yb_ref, mod0_ref, mod1_ref, g_ref, wo_ref, wi_ref,
                x1_ref, q_ref, k_ref, v_ref, sg_ref):
    cw, aw = CONV_WIDTH, ATTN_WIDTH
    y = (jnp.dot(ya_ref[0].astype(BF16), wo_ref[0:cw, :], preferred_element_type=F32)
         + jnp.dot(yb_ref[0].astype(BF16), wo_ref[cw:, :], preferred_element_type=F32))
    x1 = x_ref[0] + mod0_ref[0][2:3] * y
    x1_ref[0] = x1
    mod1 = mod1_ref[0]
    h = _rms_modulate(x1, g_ref[...], mod1[1:2], mod1[0:1]).astype(BF16)
    scale = ATTN_HEAD_DIM ** -0.5 * LOG2_E
    q_ref[0] = (jnp.dot(h, wi_ref[:, 0:aw], preferred_element_type=F32) * scale).astype(BF16)
    k_ref[0] = jnp.dot(h, wi_ref[:, aw:2 * aw], preferred_element_type=F32).astype(BF16)
    v_ref[0] = jnp.dot(h, wi_ref[:, 2 * aw:3 * aw], preferred_element_type=F32).astype(BF16)
    sg_ref[0] = _silu(jnp.dot(h, wi_ref[:, 3 * aw:4 * aw], preferred_element_type=F32))


def _mid(x, ya, yb, mod0, mod1, norm_g, w_out, w_in):
    bsz, seq, d = x.shape
    tl = ROW_TILE
    row = lambda width: pl.BlockSpec((1, tl, width), lambda b, i: (b, i, 0))
    modspec = pl.BlockSpec((1, 3, d), lambda b, i: (b, 0, 0))
    full = lambda r, c: pl.BlockSpec((r, c), lambda b, i: (0, 0))
    aw = ATTN_WIDTH
    return pl.pallas_call(
        _mid_kernel,
        grid=(bsz, seq // tl),
        in_specs=[row(d), row(CONV_WIDTH), row(SSM_WIDTH), modspec, modspec, full(1, d),
                  full(CONV_WIDTH + SSM_WIDTH, d), full(d, 4 * aw)],
        out_specs=[row(d), row(aw), row(aw), row(aw), row(aw)],
        out_shape=[jax.ShapeDtypeStruct((bsz, seq, d), F32),
                   jax.ShapeDtypeStruct((bsz, seq, aw), BF16),
                   jax.ShapeDtypeStruct((bsz, seq, aw), BF16),
                   jax.ShapeDtypeStruct((bsz, seq, aw), BF16),
                   jax.ShapeDtypeStruct((bsz, seq, aw), F32)],
        compiler_params=pltpu.CompilerParams(
            dimension_semantics=("parallel", "parallel"), vmem_limit_bytes=VMEM_LIMIT_BYTES),
    )(x, ya, yb, mod0, mod1, norm_g.reshape(1, d), w_out.astype(BF16), w_in.astype(BF16))


def _attn_kernel(q_ref, k_ref, v_ref, o_ref, acc_ref, carry_ref, qn_ref, sp0_ref, sp1_ref, e0_ref, e1_ref):
    blk = ATTN_BLOCK
    pair = 2 * ATTN_HEAD_DIM
    heads = range(2 * ATTN_PAIRS)
    cols = [slice((s // 2) * pair, (s // 2 + 1) * pair) for s in heads]
    qi = pl.program_id(2)
    lane = lax.broadcasted_iota(jnp.int32, (blk, pair), 1)
    first_head = lane < ATTN_HEAD_DIM
    r_idx = lax.broadcasted_iota(jnp.int32, (blk, blk), 0)
    c_idx = lax.broadcasted_iota(jnp.int32, (blk, blk), 1)
    not_before = (r_idx >= c_idx).astype(BF16)
    causal = c_idx < r_idx

    for s in heads:
        q = q_ref[0, :, cols[s]].astype(F32)
        qn_ref[s] = jnp.where(first_head == (s % 2 == 0), -q, 0.0).astype(BF16)
    acc_ref[...] = jnp.zeros_like(acc_ref)
    carry_ref[...] = jnp.zeros_like(carry_ref)

    def a_matmul(start):
        return [lax.dot_general(qn_ref[s], k_ref[0, pl.ds(start, blk), cols[s]],
                                (((1,), (1,)), ((), ())), preferred_element_type=F32) for s in heads]

    def a_rest(zns, diagonal, sp_ref, e_ref):
        for s in heads:
            zn = zns[s]
            l = jnp.minimum(zn, 0.0) - jnp.log(1.0 + jnp.exp2(-jnp.abs(zn))) * LOG2_E
            if diagonal:
                l = jnp.where(causal, l, 0.0)
                zn = jnp.where(causal, zn, MASKED_LOGIT)
            sp_ref[s] = l.astype(BF16)
            e_ref[s] = zn

    def b_matmul(sp_ref):
        return [jnp.dot(sp_ref[s], not_before, preferred_element_type=F32) for s in heads]

    def b_rest(sums, start, e_ref):
        for s in heads:
            w = jnp.exp2((sums[s] + carry_ref[s]) - e_ref[s])
            acc_ref[s] += jnp.dot(w.astype(BF16), v_ref[0, pl.ds(start, blk), cols[s]],
                                  preferred_element_type=F32)
            carry_ref[s] += sums[s][:, 0:1]

    def step(a_start, src, dst):
        zns = a_matmul(a_start)
        sums = b_matmul(src[0])
        a_rest(zns, False, *dst)
        b_rest(sums, a_start + blk, src[1])

    buf0, buf1 = (sp0_ref, e0_ref), (sp1_ref, e1_ref)
    diag_start = pl.multiple_of(qi * blk, blk)
    odd = qi % 2 == 1

    @pl.when(odd)
    def _():
        a_rest(a_matmul(diag_start), True, *buf1)
        step(pl.multiple_of(diag_start - blk, blk), buf1, buf0)

    @pl.when(jnp.logical_not(odd))
    def _():
        a_rest(a_matmul(diag_start), True, *buf0)

    top = qi - qi % 2

    def pair_body(p, _):
        first = pl.multiple_of((top - 2 * p - 1) * blk, blk)
        step(first, buf0, buf1)
        step(pl.multiple_of(first - blk, blk), buf1, buf0)
        return 0

    lax.fori_loop(0, qi // 2, pair_body, 0)
    b_rest(b_matmul(sp0_ref), 0, e0_ref)
    for p in range(ATTN_PAIRS):
        o_ref[0, :, cols[2 * p]] = jnp.where(first_head, acc_ref[2 * p], acc_ref[2 * p + 1])


def _attention(q, k, v):
    bsz, seq, aw = q.shape
    pair = 2 * ATTN_HEAD_DIM
    nh = 2 * ATTN_PAIRS
    width = ATTN_PAIRS * pair
    blk = ATTN_BLOCK
    return pl.pallas_call(
        _attn_kernel,
        grid=(bsz, aw // width, seq // blk),
        in_specs=[pl.BlockSpec((1, blk, width), lambda b, h, i: (b, i, h)),
                  pl.BlockSpec((1, seq, width), lambda b, h, i: (b, 0, h)),
                  pl.BlockSpec((1, seq, width), lambda b, h, i: (b, 0, h))],
        out_specs=pl.BlockSpec((1, blk, width), lambda b, h, i: (b, i, h)),
        out_shape=jax.ShapeDtypeStruct((bsz, seq, aw), F32),
        scratch_shapes=[pltpu.VMEM((nh, blk, pair), F32),
                        pltpu.VMEM((nh, blk, 1), F32),
                        pltpu.VMEM((nh, blk, pair), BF16),
                        pltpu.VMEM((nh, blk, blk), BF16), pltpu.VMEM((nh, blk, blk), BF16),
                        pltpu.VMEM((nh, blk, blk), F32), pltpu.VMEM((nh, blk, blk), F32)],
        compiler_params=pltpu.CompilerParams(
            dimension_semantics=("parallel", "parallel", "parallel")),
    )(q, k, v)


def _final_kernel(x_ref, o_ref, sg_ref, mod_ref, wo_ref, g_ref, out_ref):
    y = _bdot(o_ref[0] * sg_ref[0], wo_ref[...])
    x2 = x_ref[0] + mod_ref[0][2:3] * y
    ms = jnp.mean(x2 * x2, axis=-1, keepdims=True)
    out_ref[0] = x2 * lax.rsqrt(ms + EPS) * g_ref[...]


def _final(x1, o, sg, mod1, w_out, final_g):
    bsz, seq, d = x1.shape
    tl = ROW_TILE
    row = pl.BlockSpec((1, tl, d), lambda b, i: (b, i, 0))
    return pl.pallas_call(
        _final_kernel,
        grid=(bsz, seq // tl),
        in_specs=[row, row, row,
                  pl.BlockSpec((1, 3, d), lambda b, i: (b, 0, 0)),
                  pl.BlockSpec((ATTN_WIDTH, d), lambda b, i: (0, 0)),
                  pl.BlockSpec((1, d), lambda b, i: (0, 0))],
        out_specs=row,
        out_shape=jax.ShapeDtypeStruct((bsz, seq, d), F32),
        compiler_params=pltpu.CompilerParams(dimension_semantics=("parallel", "parallel")),
    )(x1, o, sg, mod1, w_out.astype(BF16), final_g.reshape(1, d))


def kernel(x, c, l0_norm_g, l0_w_ada, l0_b_ada, l0_w_in, l0_conv_w, l0_conv_b, l0_conv_ln_g, l0_conv_ln_b, l0_ssm_lam_re, l0_ssm_lam_im, l0_ssm_log_dt, l0_ssm_b_re, l0_ssm_b_im, l0_ssm_c_re, l0_ssm_c_im, l0_ssm_d, l0_ssm_w_glu, l0_ssm_b_glu, l0_w_out, l1_norm_g, l1_w_ada, l1_b_ada, l1_w_in, l1_w_out, final_norm_g):
    bsz, seq, d = x.shape
    mod0 = _ada(c, l0_w_ada, l0_b_ada).reshape(bsz, 3, d)
    mod1 = _ada(c, l1_w_ada, l1_b_ada).reshape(bsz, 3, d)

    hg, ga, u, gb = _even_in(x, mod0, l0_norm_g, l0_w_in)
    ya = _conv_branch(hg, ga, l0_conv_w, l0_conv_b, l0_conv_ln_g, l0_conv_ln_b)

    pre, pim, bbre, bbim = _ssm_prep(l0_ssm_lam_re, l0_ssm_lam_im, l0_ssm_log_dt, l0_ssm_b_re, l0_ssm_b_im)
    yb = _ssm_branch(u, gb, _block_diag_in(bbre), _block_diag_in(bbim), pre, pim,
                     _block_diag_out(l0_ssm_c_re), _block_diag_out(l0_ssm_c_im),
                     l0_ssm_d, l0_ssm_w_glu, l0_ssm_b_glu)

    x1, q, k, v, sg = _mid(x, ya, yb, mod0, mod1, l1_norm_g, l0_w_out, l1_w_in)
    o = _attention(q, k, v)
    return _final(x1, o, sg, mod1, l1_w_out, final_norm_g)
```

```python
import functools
import math

import jax
import jax.numpy as jnp
from jax import lax
import numpy as np
from jax.experimental import pallas as pl
from jax.experimental.pallas import tpu as pltpu

F32 = jnp.float32
BF16 = jnp.bfloat16

D_MODEL = 1024
CONV_WIDTH = 1024
CONV_KERNEL = 31
SSM_WIDTH = 512
SSM_GROUP = 16
SSM_GROUPS = SSM_WIDTH // SSM_GROUP
SSM_STATE = 64
SSM_LANES = SSM_GROUPS * SSM_STATE
ATTN_HEADS = 16
ATTN_HEAD_DIM = 64
ATTN_WIDTH = ATTN_HEADS * ATTN_HEAD_DIM
EPS = 1e-6
LOG2_E = math.log2(math.e)
MASKED_LOGIT = 1e30

SUBLANES = 8
LANES = 128
VMEM_LIMIT_BYTES = 56 * 1024 * 1024

ROW_TILE = 256
CONV_TILE = 128
CONV_HALO = 32
CONV_CHUNK = 32
SSM_CHUNK = 32
SSM_SCAN_TILES = 4
SSM_SCAN_UNROLL = 4
ATTN_BLOCK = 256
ATTN_PAIRS = 2


def _silu(x):
    return x * jax.nn.sigmoid(x)


def _bdot(a, b):
    return jnp.dot(a.astype(BF16), b.astype(BF16), preferred_element_type=F32)


def _rms_modulate(x, g, scale, shift):
    ms = jnp.mean(x * x, axis=-1, keepdims=True)
    return (x * lax.rsqrt(ms + EPS) * g) * (1.0 + scale) + shift


def _ada_kernel(c_ref, w_ref, b_ref, o_ref):
    o_ref[...] = _bdot(_silu(c_ref[...]), w_ref[...]) + b_ref[...]


def _ada(c, w_ada, b_ada):
    bsz, d = c.shape
    n = w_ada.shape[1]
    tn = d
    return pl.pallas_call(
        _ada_kernel,
        grid=(n // tn,),
        in_specs=[pl.BlockSpec((bsz, d), lambda j: (0, 0)),
                  pl.BlockSpec((d, tn), lambda j: (0, j)),
                  pl.BlockSpec((1, tn), lambda j: (0, j))],
        out_specs=pl.BlockSpec((bsz, tn), lambda j: (0, j)),
        out_shape=jax.ShapeDtypeStruct((bsz, n), F32),
        compiler_params=pltpu.CompilerParams(dimension_semantics=("parallel",)),
    )(c, w_ada, b_ada.reshape(1, n))


def _even_in_kernel(x_ref, mod_ref, g_ref, w_ref, hg_ref, ga_ref, u_ref, gb_ref):
    mod = mod_ref[0]
    h = _rms_modulate(x_ref[0], g_ref[...], mod[1:2], mod[0:1]).astype(BF16)
    cw, sw = CONV_WIDTH, SSM_WIDTH
    val = jnp.dot(h, w_ref[:, 0:cw], preferred_element_type=F32)
    glu = jnp.dot(h, w_ref[:, cw:2 * cw], preferred_element_type=F32)
    hg_ref[0] = val * jax.nn.sigmoid(glu)
    ga_ref[0] = _silu(jnp.dot(h, w_ref[:, 2 * cw:3 * cw], preferred_element_type=F32))
    u_ref[0] = jnp.dot(h, w_ref[:, 3 * cw:3 * cw + sw], preferred_element_type=F32)
    gb_ref[0] = _silu(jnp.dot(h, w_ref[:, 3 * cw + sw:3 * cw + 2 * sw], preferred_element_type=F32))


def _even_in(x, mod, norm_g, w_in):
    bsz, seq, d = x.shape
    tl = ROW_TILE
    n_in = w_in.shape[1]
    row = lambda width: pl.BlockSpec((1, tl, width), lambda b, i: (b, i, 0))
    return pl.pallas_call(
        _even_in_kernel,
        grid=(bsz, seq // tl),
        in_specs=[row(d),
                  pl.BlockSpec((1, 3, d), lambda b, i: (b, 0, 0)),
                  pl.BlockSpec((1, d), lambda b, i: (0, 0)),
                  pl.BlockSpec((d, n_in), lambda b, i: (0, 0))],
        out_specs=[row(CONV_WIDTH), row(CONV_WIDTH), row(SSM_WIDTH), row(SSM_WIDTH)],
        out_shape=[jax.ShapeDtypeStruct((bsz, seq, CONV_WIDTH), F32),
                   jax.ShapeDtypeStruct((bsz, seq, CONV_WIDTH), F32),
                   jax.ShapeDtypeStruct((bsz, seq, SSM_WIDTH), F32),
                   jax.ShapeDtypeStruct((bsz, seq, SSM_WIDTH), F32)],
        compiler_params=pltpu.CompilerParams(
            dimension_semantics=("parallel", "parallel"), vmem_limit_bytes=VMEM_LIMIT_BYTES),
    )(x, mod, norm_g.reshape(1, d), w_in.astype(BF16))


def _conv_kernel(cur_ref, halo_ref, ga_ref, w_ref, cb_ref, lg_ref, lb_ref, o_ref, buf_ref, sh_ref):
    tl = cur_ref.shape[1]
    first = pl.program_id(1) == 0
    buf_ref[0:CONV_HALO, :] = jnp.where(first, 0.0, halo_ref[0])
    buf_ref[CONV_HALO:CONV_HALO + tl, :] = cur_ref[0]
    lead = CONV_HALO - (CONV_KERNEL - 1)
    rows = sh_ref.shape[1]
    for r in range(1, SUBLANES):
        sh_ref[r - 1] = buf_ref[r:r + rows, :]

    groups = CONV_CHUNK // SUBLANES
    for r0 in range(0, tl, CONV_CHUNK):
        acc = jnp.zeros((groups, SUBLANES, CONV_WIDTH), F32)
        for k in range(CONV_KERNEL):
            r, base = (lead + k) % SUBLANES, r0 + (lead + k) // SUBLANES * SUBLANES
            src = buf_ref if r == 0 else sh_ref.at[r - 1]
            rows8 = src[base:base + CONV_CHUNK, :].reshape(groups, SUBLANES, CONV_WIDTH)
            acc = acc + rows8 * w_ref[k * SUBLANES:(k + 1) * SUBLANES, :][None]
        acc = acc.reshape(CONV_CHUNK, CONV_WIDTH) + cb_ref[...]
        mu = jnp.mean(acc, axis=-1, keepdims=True)
        cen = acc - mu
        var = jnp.mean(cen * cen, axis=-1, keepdims=True)
        y = cen * lax.rsqrt(var + EPS) * lg_ref[...] + lb_ref[...]
        o_ref[0, r0:r0 + CONV_CHUNK, :] = _silu(y) * ga_ref[0, r0:r0 + CONV_CHUNK, :]


def _conv_branch(hg, ga, conv_w, conv_b, ln_g, ln_b):
    bsz, seq, cw = hg.shape
    tl = CONV_TILE
    halo_per_tile = tl // CONV_HALO
    row = pl.BlockSpec((1, tl, cw), lambda b, i: (b, i, 0))
    vec = pl.BlockSpec((1, cw), lambda b, i: (0, 0))
    return pl.pallas_call(
        _conv_kernel,
        grid=(bsz, seq // tl),
        in_specs=[row,
                  pl.BlockSpec((1, CONV_HALO, cw),
                               lambda b, i: (b, jnp.maximum(i * halo_per_tile - 1, 0), 0)),
                  row,
                  pl.BlockSpec((CONV_KERNEL * SUBLANES, cw), lambda b, i: (0, 0)),
                  vec, vec, vec],
        out_specs=row,
        out_shape=jax.ShapeDtypeStruct((bsz, seq, cw), F32),
        scratch_shapes=[pltpu.VMEM((CONV_HALO + tl, cw), F32),
                        pltpu.VMEM((SUBLANES - 1, CONV_HALO + tl - SUBLANES, cw), F32)],
        compiler_params=pltpu.CompilerParams(dimension_semantics=("parallel", "parallel")),
    )(hg, hg, ga, jnp.repeat(conv_w, SUBLANES, axis=0), conv_b.reshape(1, cw), ln_g.reshape(1, cw),
      ln_b.reshape(1, cw))


def _ssm_prep_kernel(lre_ref, lim_ref, ldt_ref, bre_ref, bim_ref, pre_ref, pim_ref, bbre_ref, bbim_ref):
    lre, lim = lre_ref[...], lim_ref[...]
    dt = jnp.exp(ldt_ref[...])
    mag = jnp.exp(lre * dt)
    are, aim = mag * jnp.cos(lim * dt), mag * jnp.sin(lim * dt)
    nre, nim = are - 1.0, aim
    den = lre * lre + lim * lim
    cre = (nre * lre + nim * lim) / den
    cim = (nim * lre - nre * lim) / den
    bre, bim = bre_ref[...], bim_ref[...]
    bbre_ref[...] = cre * bre - cim * bim
    bbim_ref[...] = cre * bim + cim * bre
    pre_ref[...] = are
    pim_ref[...] = aim


def _ssm_prep(lam_re, lam_im, log_dt, b_re, b_im):
    n = SSM_LANES
    flat = lambda t: t.reshape(1, n)
    to_lanes = lambda t: t.reshape(n, SSM_GROUP).T
    ldt = jnp.broadcast_to(log_dt[:, None], (SSM_GROUPS, SSM_STATE))
    outs = pl.pallas_call(
        _ssm_prep_kernel,
        out_shape=[jax.ShapeDtypeStruct((1, n), F32), jax.ShapeDtypeStruct((1, n), F32),
                   jax.ShapeDtypeStruct((SSM_GROUP, n), F32), jax.ShapeDtypeStruct((SSM_GROUP, n), F32)],
    )(flat(lam_re), flat(lam_im), flat(ldt), to_lanes(b_re), to_lanes(b_im))
    return outs


def _group_mask_in():
    r = jnp.arange(SSM_WIDTH)[:, None] // SSM_GROUP
    c = jnp.arange(SSM_LANES)[None, :] // SSM_STATE
    return r == c


def _block_diag_in(bb):
    return jnp.where(_group_mask_in(), jnp.tile(bb, (SSM_GROUPS, 1)), 0.0).astype(BF16)


def _block_diag_out(cmat):
    per_lane = cmat.transpose(0, 2, 1).reshape(SSM_LANES, SSM_GROUP)
    return jnp.where(_group_mask_in().T, jnp.tile(per_lane, (1, SSM_GROUPS)), 0.0).astype(BF16)


def _permute_rows_f32(perm, x):
    hi = x.astype(BF16)
    rest = x - hi.astype(F32)
    mid = rest.astype(BF16)
    lo = (rest - mid.astype(F32)).astype(BF16)
    return (jnp.dot(perm, hi, preferred_element_type=F32) + jnp.dot(perm, mid, preferred_element_type=F32)
            + jnp.dot(perm, lo, preferred_element_type=F32))


def _ssm_kernel(u_ref, gb_ref, wbre_ref, wbim_ref, pre_ref, pim_ref, wcre_ref, wcim_ref,
                d_ref, wg_ref, bg_ref, o_ref, hre_ref, him_ref, sre_ref, sim_ref):
    bsz, tc, sw = u_ref.shape
    rows = bsz * tc

    @pl.when(pl.program_id(0) == 0)
    def _():
        sre_ref[...] = jnp.zeros_like(sre_ref)
        sim_ref[...] = jnp.zeros_like(sim_ref)

    u = u_ref[...].reshape(rows, sw)
    r_idx = lax.broadcasted_iota(jnp.int32, (rows, rows), 0)
    c_idx = lax.broadcasted_iota(jnp.int32, (rows, rows), 1)
    to_time_major = (c_idx == (r_idx % bsz) * tc + r_idx // bsz).astype(BF16)
    to_batch_major = (c_idx == (r_idx % tc) * bsz + r_idx // tc).astype(BF16)
    ub = jnp.dot(to_time_major, u.astype(BF16), preferred_element_type=F32).astype(BF16)
    hw, hn = SSM_WIDTH // 2, SSM_LANES // 2
    for half in range(2):
        chans, lns = slice(half * hw, (half + 1) * hw), slice(half * hn, (half + 1) * hn)
        hre_ref[:, lns] = jnp.dot(ub[:, chans], wbre_ref[chans, lns], preferred_element_type=F32)
        him_ref[:, lns] = jnp.dot(ub[:, chans], wbim_ref[chans, lns], preferred_element_type=F32)

    width = SSM_SCAN_TILES * LANES
    for j0 in range(0, SSM_LANES, width):
        ls = slice(j0, j0 + width)
        are = jnp.broadcast_to(pre_ref[:, ls], (bsz, width))
        aim = jnp.broadcast_to(pim_ref[:, ls], (bsz, width))

        def tick(t, h):
            hr, hi = h
            r0 = pl.multiple_of(t * bsz, bsz)
            nr = (are * hr - aim * hi) + hre_ref[pl.ds(r0, bsz), ls]
            ni = (are * hi + aim * hr) + him_ref[pl.ds(r0, bsz), ls]
            hre_ref[pl.ds(r0, bsz), ls] = nr
            him_ref[pl.ds(r0, bsz), ls] = ni
            return nr, ni

        hr, hi = lax.fori_loop(0, tc, tick, (sre_ref[:, ls], sim_ref[:, ls]), unroll=SSM_SCAN_UNROLL)
        sre_ref[:, ls] = hr
        sim_ref[:, ls] = hi

    ys = []
    for half in range(2):
        cols, lns = slice(half * hw, (half + 1) * hw), slice(half * hn, (half + 1) * hn)
        ys.append(jnp.dot(hre_ref[:, lns].astype(BF16), wcre_ref[lns, cols], preferred_element_type=F32)
                  - jnp.dot(him_ref[:, lns].astype(BF16), wcim_ref[lns, cols], preferred_element_type=F32))
    y = _permute_rows_f32(to_batch_major, jnp.concatenate(ys, axis=1))
    y = y + d_ref[...] * u
    y = jax.nn.gelu(y)
    y = y * jax.nn.sigmoid(_bdot(y, wg_ref[...]) + bg_ref[...])
    o_ref[...] = (y * gb_ref[...].reshape(rows, sw)).reshape(bsz, tc, sw)


def _ssm_branch(u, gb, wbre, wbim, pre, pim, wcre, wcim, d_skip, w_glu, b_glu):
    bsz, seq, sw = u.shape
    assert bsz == SUBLANES, "the scan keeps the batch on the sublanes of one vreg tile"
    tc = SSM_CHUNK
    n = SSM_LANES
    row = pl.BlockSpec((bsz, tc, sw), lambda i: (0, i, 0))
    full = lambda r, c: pl.BlockSpec((r, c), lambda i: (0, 0))
    return pl.pallas_call(
        _ssm_kernel,
        grid=(seq // tc,),
        in_specs=[row, row, full(sw, n), full(sw, n), full(1, n), full(1, n),
                  full(n, sw), full(n, sw), full(1, sw), full(sw, sw), full(1, sw)],
        out_specs=row,
        out_shape=jax.ShapeDtypeStruct((bsz, seq, sw), F32),
        scratch_shapes=[pltpu.VMEM((bsz * tc, n), F32), pltpu.VMEM((bsz * tc, n), F32),
                        pltpu.VMEM((bsz, n), F32), pltpu.VMEM((bsz, n), F32)],
        compiler_params=pltpu.CompilerParams(
            dimension_semantics=("arbitrary",), vmem_limit_bytes=VMEM_LIMIT_BYTES),
    )(u, gb, wbre, wbim, pre, pim, wcre, wcim, d_skip.reshape(1, sw), w_glu.astype(BF16),
      b_glu.reshape(1, sw))


def _mid_kernel(x_ref, ya_ref, yb_ref, mod0_ref, mod1_ref, g_ref, wo_ref, wi_ref,
                x1_ref, q_ref, k_ref, v_ref, sg_ref):
    cw, aw = CONV_WIDTH, ATTN_WIDTH
    y = (jnp.dot(ya_ref[0].astype(BF16), wo_ref[0:cw, :], preferred_element_type=F32)
         + jnp.dot(yb_ref[0].astype(BF16), wo_ref[cw:, :], preferred_element_type=F32))
    x1 = x_ref[0] + mod0_ref[0][2:3] * y
    x1_ref[0] = x1
    mod1 = mod1_ref[0]
    h = _rms_modulate(x1, g_ref[...], mod1[1:2], mod1[0:1]).astype(BF16)
    scale = ATTN_HEAD_DIM ** -0.5 * LOG2_E
    q_ref[0] = (jnp.dot(h, wi_ref[:, 0:aw], preferred_element_type=F32) * scale).astype(BF16)
    k_ref[0] = jnp.dot(h, wi_ref[:, aw:2 * aw], preferred_element_type=F32).astype(BF16)
    v_ref[0] = jnp.dot(h, wi_ref[:, 2 * aw:3 * aw], preferred_element_type=F32).astype(BF16)
    sg_ref[0] = _silu(jnp.dot(h, wi_ref[:, 3 * aw:4 * aw], preferred_element_type=F32))


def _mid(x, ya, yb, mod0, mod1, norm_g, w_out, w_in):
    bsz, seq, d = x.shape
    tl = ROW_TILE
    row = lambda width: pl.BlockSpec((1, tl, width), lambda b, i: (b, i, 0))
    modspec = pl.BlockSpec((1, 3, d), lambda b, i: (b, 0, 0))
    full = lambda r, c: pl.BlockSpec((r, c), lambda b, i: (0, 0))
    aw = ATTN_WIDTH
    return pl.pallas_call(
        _mid_kernel,
        grid=(bsz, seq // tl),
        in_specs=[row(d), row(CONV_WIDTH), row(SSM_WIDTH), modspec, modspec, full(1, d),
                  full(CONV_WIDTH + SSM_WIDTH, d), full(d, 4 * aw)],
        out_specs=[row(d), row(aw), row(aw), row(aw), row(aw)],
        out_shape=[jax.ShapeDtypeStruct((bsz, seq, d), F32),
                   jax.ShapeDtypeStruct((bsz, seq, aw), BF16),
                   jax.ShapeDtypeStruct((bsz, seq, aw), BF16),
                   jax.ShapeDtypeStruct((bsz, seq, aw), BF16),
                   jax.ShapeDtypeStruct((bsz, seq, aw), F32)],
        compiler_params=pltpu.CompilerParams(
            dimension_semantics=("parallel", "parallel"), vmem_limit_bytes=VMEM_LIMIT_BYTES),
    )(x, ya, yb, mod0, mod1, norm_g.reshape(1, d), w_out.astype(BF16), w_in.astype(BF16))


def _attn_schedule(nq):
    visits = []
    for q in range(nq):
        visits.append((q, q, False))
        visits.append((q, q - 1, False) if q >= 1 else (0, 0, True))
    for q in range(2, nq):
        visits.extend((q, j, False) for j in range(q - 2, -1, -1))
    visits.append((0, 0, True))
    if len(visits) % 2:
        visits.append((0, 0, True))
    rows = []
    prev = (0, 0, True)
    for q, j, dummy in visits:
        pq, pj, pdummy = prev
        rows.append((q, j, nq if pdummy else pq, pj, int(pdummy)))
        prev = (q, j, dummy)
    return np.asarray(rows, np.int32).T.copy(), 2 * nq, len(visits)


def _attn_kernel(tbl_ref, q_ref, k_ref, v_ref, o_ref, acc_ref, carry_ref, qn_ref, qp_ref,
                 sp0_ref, sp1_ref, e0_ref, e1_ref, *, diag_steps, total_steps):
    blk = ATTN_BLOCK
    pair = 2 * ATTN_HEAD_DIM
    heads = range(2 * ATTN_PAIRS)
    cols = [slice((s // 2) * pair, (s // 2 + 1) * pair) for s in heads]
    nq = q_ref.shape[1] // blk
    lane = lax.broadcasted_iota(jnp.int32, (blk, pair), 1)
    first_head = lane < ATTN_HEAD_DIM
    r_idx = lax.broadcasted_iota(jnp.int32, (blk, blk), 0)
    c_idx = lax.broadcasted_iota(jnp.int32, (blk, blk), 1)
    not_before = (r_idx >= c_idx).astype(BF16)
    causal = c_idx < r_idx
    nt = (((1,), (1,)), ((), ()))

    for qb in range(nq):
        for s in heads:
            q = q_ref[0, qb * blk:(qb + 1) * blk, cols[s]].astype(F32)
            own = first_head == (s % 2 == 0)
            qn_ref[qb, s] = jnp.where(own, -q, 0.0).astype(BF16)
            qp_ref[qb, s] = jnp.where(own, q, 0.0).astype(BF16)
    acc_ref[...] = jnp.zeros_like(acc_ref)
    carry_ref[...] = jnp.zeros_like(carry_ref)
    sp1_ref[...] = jnp.zeros_like(sp1_ref)
    e1_ref[...] = jnp.zeros_like(e1_ref)

    def step(i, diagonal, sp_a, e_a, sp_b, e_b):
        a_q, a_key = tbl_ref[0, i], tbl_ref[1, i]
        b_slot, b_key, b_dummy = tbl_ref[2, i], tbl_ref[3, i], tbl_ref[4, i]
        a_start = pl.multiple_of(a_key * blk, blk)
        b_start = pl.multiple_of(b_key * blk, blk)
        penalty = jnp.where(b_dummy == 1, -MASKED_LOGIT, 0.0)
        zs = [(lax.dot_general(qn_ref[a_q, s], k_ref[0, pl.ds(a_start, blk), cols[s]], nt,
                               preferred_element_type=F32),
               lax.dot_general(qp_ref[a_q, s], k_ref[0, pl.ds(a_start, blk), cols[s]], nt,
                               preferred_element_type=F32)) for s in heads]
        sums = [jnp.dot(sp_b[s], not_before, preferred_element_type=F32) for s in heads]
        for s in heads:
            zn, zp = zs[s]
            l = jnp.minimum(zn, 0.0) - jnp.log(1.0 + jnp.exp2(jnp.minimum(zn, zp))) * LOG2_E
            if diagonal:
                l = jnp.where(causal, l, 0.0)
                zn = jnp.where(causal, zn, MASKED_LOGIT)
            sp_a[s] = l.astype(BF16)
            e_a[s] = zn
        for s in heads:
            w = jnp.exp2((sums[s] + (carry_ref[b_slot, s] + penalty)) - e_b[s])
            acc_ref[b_slot, s] += jnp.dot(w.astype(BF16), v_ref[0, pl.ds(b_start, blk), cols[s]],
                                          preferred_element_type=F32)
            carry_ref[b_slot, s] += sums[s][:, 0:1]

    def diag_pair(p, _):
        step(2 * p, True, sp0_ref, e0_ref, sp1_ref, e1_ref)
        step(2 * p + 1, False, sp1_ref, e1_ref, sp0_ref, e0_ref)
        return 0

    def plain_pair(p, _):
        step(2 * p, False, sp0_ref, e0_ref, sp1_ref, e1_ref)
        step(2 * p + 1, False, sp1_ref, e1_ref, sp0_ref, e0_ref)
        return 0

    lax.fori_loop(0, diag_steps // 2, diag_pair, 0)
    lax.fori_loop(diag_steps // 2, total_steps // 2, plain_pair, 0)
    for qb in range(nq):
        for p in range(ATTN_PAIRS):
            o_ref[0, qb * blk:(qb + 1) * blk, cols[2 * p]] = jnp.where(
                first_head, acc_ref[qb, 2 * p], acc_ref[qb, 2 * p + 1])


def _attention(q, k, v):
    bsz, seq, aw = q.shape
    pair = 2 * ATTN_HEAD_DIM
    nh = 2 * ATTN_PAIRS
    width = ATTN_PAIRS * pair
    blk = ATTN_BLOCK
    nq = seq // blk
    table, diag_steps, total_steps = _attn_schedule(nq)
    assert total_steps % 2 == 0 and table.shape[1] == total_steps
    seq_block = pl.BlockSpec((1, seq, width), lambda b, h, tbl: (b, 0, h))
    return pl.pallas_call(
        functools.partial(_attn_kernel, diag_steps=diag_steps, total_steps=total_steps),
        grid_spec=pltpu.PrefetchScalarGridSpec(
            num_scalar_prefetch=1,
            grid=(bsz, aw // width),
            in_specs=[seq_block, seq_block, seq_block],
            out_specs=seq_block,
            scratch_shapes=[pltpu.VMEM((nq + 1, nh, blk, pair), F32),
                            pltpu.VMEM((nq + 1, nh, blk, 1), F32),
                            pltpu.VMEM((nq, nh, blk, pair), BF16),
                            pltpu.VMEM((nq, nh, blk, pair), BF16),
                            pltpu.VMEM((nh, blk, blk), BF16), pltpu.VMEM((nh, blk, blk), BF16),
                            pltpu.VMEM((nh, blk, blk), F32), pltpu.VMEM((nh, blk, blk), F32)]),
        out_shape=jax.ShapeDtypeStruct((bsz, seq, aw), F32),
        compiler_params=pltpu.CompilerParams(
            dimension_semantics=("parallel", "parallel"), vmem_limit_bytes=VMEM_LIMIT_BYTES),
    )(jnp.asarray(table), q, k, v)


def _final_kernel(x_ref, o_ref, sg_ref, mod_ref, wo_ref, g_ref, out_ref):
    y = _bdot(o_ref[0] * sg_ref[0], wo_ref[...])
    x2 = x_ref[0] + mod_ref[0][2:3] * y
    ms = jnp.mean(x2 * x2, axis=-1, keepdims=True)
    out_ref[0] = x2 * lax.rsqrt(ms + EPS) * g_ref[...]


def _final(x1, o, sg, mod1, w_out, final_g):
    bsz, seq, d = x1.shape
    tl = ROW_TILE
    row = pl.BlockSpec((1, tl, d), lambda b, i: (b, i, 0))
    return pl.pallas_call(
        _final_kernel,
        grid=(bsz, seq // tl),
        in_specs=[row, row, row,
                  pl.BlockSpec((1, 3, d), lambda b, i: (b, 0, 0)),
                  pl.BlockSpec((ATTN_WIDTH, d), lambda b, i: (0, 0)),
                  pl.BlockSpec((1, d), lambda b, i: (0, 0))],
        out_specs=row,
        out_shape=jax.ShapeDtypeStruct((bsz, seq, d), F32),
        compiler_params=pltpu.CompilerParams(dimension_semantics=("parallel", "parallel")),
    )(x1, o, sg, mod1, w_out.astype(BF16), final_g.reshape(1, d))


def kernel(x, c, l0_norm_g, l0_w_ada, l0_b_ada, l0_w_in, l0_conv_w, l0_conv_b, l0_conv_ln_g, l0_conv_ln_b, l0_ssm_lam_re, l0_ssm_lam_im, l0_ssm_log_dt, l0_ssm_b_re, l0_ssm_b_im, l0_ssm_c_re, l0_ssm_c_im, l0_ssm_d, l0_ssm_w_glu, l0_ssm_b_glu, l0_w_out, l1_norm_g, l1_w_ada, l1_b_ada, l1_w_in, l1_w_out, final_norm_g):
    bsz, seq, d = x.shape
    mod0 = _ada(c, l0_w_ada, l0_b_ada).reshape(bsz, 3, d)
    mod1 = _ada(c, l1_w_ada, l1_b_ada).reshape(bsz, 3, d)

    hg, ga, u, gb = _even_in(x, mod0, l0_norm_g, l0_w_in)
    ya = _conv_branch(hg, ga, l0_conv_w, l0_conv_b, l0_conv_ln_g, l0_conv_ln_b)

    pre, pim, bbre, bbim = _ssm_prep(l0_ssm_lam_re, l0_ssm_lam_im, l0_ssm_log_dt, l0_ssm_b_re, l0_ssm_b_im)
    yb = _ssm_branch(u, gb, _block_diag_in(bbre), _block_diag_in(bbim), pre, pim,
                     _block_diag_out(l0_ssm_c_re), _block_diag_out(l0_ssm_c_im),
                     l0_ssm_d, l0_ssm_w_glu, l0_ssm_b_glu)

    x1, q, k, v, sg = _mid(x, ya, yb, mod0, mod1, l1_norm_g, l0_w_out, l1_w_in)
    o = _attention(q, k, v)
    return _final(x1, o, sg, mod1, l1_w_out, final_norm_g)
```

```python
import functools
import math

import jax
import jax.numpy as jnp
from jax import lax
import numpy as np
from jax.experimental import pallas as pl
from jax.experimental.pallas import tpu as pltpu

F32 = jnp.float32
BF16 = jnp.bfloat16

D_MODEL = 1024
CONV_WIDTH = 1024
CONV_KERNEL = 31
SSM_WIDTH = 512
SSM_GROUP = 16
SSM_GROUPS = SSM_WIDTH // SSM_GROUP
SSM_STATE = 64
SSM_LANES = SSM_GROUPS * SSM_STATE
ATTN_HEADS = 16
ATTN_HEAD_DIM = 64
ATTN_WIDTH = ATTN_HEADS * ATTN_HEAD_DIM
EPS = 1e-6
LOG2_E = math.log2(math.e)
MASKED_LOGIT = 1e30

SUBLANES = 8
LANES = 128
VMEM_LIMIT_BYTES = 56 * 1024 * 1024

ROW_TILE = 512
CONV_TILE = 256
CONV_HALO = 32
CONV_CHUNK = 32
SSM_CHUNK = 32
SSM_SCAN_TILES = 8
SSM_SCAN_UNROLL = 4
ATTN_BLOCK = 256
ATTN_PAIRS = 2


def _silu(x):
    return x * jax.nn.sigmoid(x)


def _bdot(a, b):
    return jnp.dot(a.astype(BF16), b.astype(BF16), preferred_element_type=F32)


def _rms_modulate(x, g, scale, shift):
    ms = jnp.mean(x * x, axis=-1, keepdims=True)
    return (x * lax.rsqrt(ms + EPS) * g) * (1.0 + scale) + shift


def _ada_kernel(c_ref, w_ref, b_ref, o_ref):
    o_ref[...] = _bdot(_silu(c_ref[...]), w_ref[...]) + b_ref[...]


def _ada(c, w_ada, b_ada):
    bsz, d = c.shape
    n = w_ada.shape[1]
    tn = d
    return pl.pallas_call(
        _ada_kernel,
        grid=(n // tn,),
        in_specs=[pl.BlockSpec((bsz, d), lambda j: (0, 0)),
                  pl.BlockSpec((d, tn), lambda j: (0, j)),
                  pl.BlockSpec((1, tn), lambda j: (0, j))],
        out_specs=pl.BlockSpec((bsz, tn), lambda j: (0, j)),
        out_shape=jax.ShapeDtypeStruct((bsz, n), F32),
        compiler_params=pltpu.CompilerParams(dimension_semantics=("parallel",)),
    )(c, w_ada, b_ada.reshape(1, n))


def _even_in_kernel(x_ref, mod_ref, g_ref, w_ref, hg_ref, ga_ref, u_ref, gb_ref):
    mod = mod_ref[0]
    h = _rms_modulate(x_ref[0], g_ref[...], mod[1:2], mod[0:1]).astype(BF16)
    cw, sw = CONV_WIDTH, SSM_WIDTH
    val = jnp.dot(h, w_ref[:, 0:cw], preferred_element_type=F32)
    glu = jnp.dot(h, w_ref[:, cw:2 * cw], preferred_element_type=F32)
    hg_ref[0] = val * jax.nn.sigmoid(glu)
    ga_ref[0] = _silu(jnp.dot(h, w_ref[:, 2 * cw:3 * cw], preferred_element_type=F32))
    u_ref[0] = jnp.dot(h, w_ref[:, 3 * cw:3 * cw + sw], preferred_element_type=F32)
    gb_ref[0] = _silu(jnp.dot(h, w_ref[:, 3 * cw + sw:3 * cw + 2 * sw], preferred_element_type=F32))


def _even_in(x, mod, norm_g, w_in):
    bsz, seq, d = x.shape
    tl = ROW_TILE
    n_in = w_in.shape[1]
    row = lambda width: pl.BlockSpec((1, tl, width), lambda b, i: (b, i, 0))
    return pl.pallas_call(
        _even_in_kernel,
        grid=(bsz, seq // tl),
        in_specs=[row(d),
                  pl.BlockSpec((1, 3, d), lambda b, i: (b, 0, 0)),
                  pl.BlockSpec((1, d), lambda b, i: (0, 0)),
                  pl.BlockSpec((d, n_in), lambda b, i: (0, 0))],
        out_specs=[row(CONV_WIDTH), row(CONV_WIDTH), row(SSM_WIDTH), row(SSM_WIDTH)],
        out_shape=[jax.ShapeDtypeStruct((bsz, seq, CONV_WIDTH), F32),
                   jax.ShapeDtypeStruct((bsz, seq, CONV_WIDTH), F32),
                   jax.ShapeDtypeStruct((bsz, seq, SSM_WIDTH), F32),
                   jax.ShapeDtypeStruct((bsz, seq, SSM_WIDTH), F32)],
        compiler_params=pltpu.CompilerParams(
            dimension_semantics=("parallel", "parallel"), vmem_limit_bytes=VMEM_LIMIT_BYTES),
    )(x, mod, norm_g.reshape(1, d), w_in.astype(BF16))


def _conv_kernel(cur_ref, halo_ref, ga_ref, w_ref, cb_ref, lg_ref, lb_ref, o_ref, buf_ref, sh_ref):
    tl = cur_ref.shape[1]
    first = pl.program_id(1) == 0
    buf_ref[0:CONV_HALO, :] = jnp.where(first, 0.0, halo_ref[0])
    buf_ref[CONV_HALO:CONV_HALO + tl, :] = cur_ref[0]
    lead = CONV_HALO - (CONV_KERNEL - 1)
    rows = sh_ref.shape[1]
    for r in range(1, SUBLANES):
        sh_ref[r - 1] = buf_ref[r:r + rows, :]

    groups = CONV_CHUNK // SUBLANES
    for r0 in range(0, tl, CONV_CHUNK):
        acc = jnp.zeros((groups, SUBLANES, CONV_WIDTH), F32)
        for k in range(CONV_KERNEL):
            r, base = (lead + k) % SUBLANES, r0 + (lead + k) // SUBLANES * SUBLANES
            src = buf_ref if r == 0 else sh_ref.at[r - 1]
            rows8 = src[base:base + CONV_CHUNK, :].reshape(groups, SUBLANES, CONV_WIDTH)
            acc = acc + rows8 * w_ref[k * SUBLANES:(k + 1) * SUBLANES, :][None]
        acc = acc.reshape(CONV_CHUNK, CONV_WIDTH) + cb_ref[...]
        mu = jnp.mean(acc, axis=-1, keepdims=True)
        cen = acc - mu
        var = jnp.mean(cen * cen, axis=-1, keepdims=True)
        y = cen * lax.rsqrt(var + EPS) * lg_ref[...] + lb_ref[...]
        o_ref[0, r0:r0 + CONV_CHUNK, :] = _silu(y) * ga_ref[0, r0:r0 + CONV_CHUNK, :]


def _conv_branch(hg, ga, conv_w, conv_b, ln_g, ln_b):
    bsz, seq, cw = hg.shape
    tl = CONV_TILE
    halo_per_tile = tl // CONV_HALO
    row = pl.BlockSpec((1, tl, cw), lambda b, i: (b, i, 0))
    vec = pl.BlockSpec((1, cw), lambda b, i: (0, 0))
    return pl.pallas_call(
        _conv_kernel,
        grid=(bsz, seq // tl),
        in_specs=[row,
                  pl.BlockSpec((1, CONV_HALO, cw),
                               lambda b, i: (b, jnp.maximum(i * halo_per_tile - 1, 0), 0)),
                  row,
                  pl.BlockSpec((CONV_KERNEL * SUBLANES, cw), lambda b, i: (0, 0)),
                  vec, vec, vec],
        out_specs=row,
        out_shape=jax.ShapeDtypeStruct((bsz, seq, cw), F32),
        scratch_shapes=[pltpu.VMEM((CONV_HALO + tl, cw), F32),
                        pltpu.VMEM((SUBLANES - 1, CONV_HALO + tl - SUBLANES, cw), F32)],
        compiler_params=pltpu.CompilerParams(dimension_semantics=("parallel", "parallel")),
    )(hg, hg, ga, jnp.repeat(conv_w, SUBLANES, axis=0), conv_b.reshape(1, cw), ln_g.reshape(1, cw),
      ln_b.reshape(1, cw))


def _ssm_prep_kernel(lre_ref, lim_ref, ldt_ref, bre_ref, bim_ref, pre_ref, pim_ref, bbre_ref, bbim_ref):
    lre, lim = lre_ref[...], lim_ref[...]
    dt = jnp.exp(ldt_ref[...])
    mag = jnp.exp(lre * dt)
    are, aim = mag * jnp.cos(lim * dt), mag * jnp.sin(lim * dt)
    nre, nim = are - 1.0, aim
    den = lre * lre + lim * lim
    cre = (nre * lre + nim * lim) / den
    cim = (nim * lre - nre * lim) / den
    bre, bim = bre_ref[...], bim_ref[...]
    bbre_ref[...] = cre * bre - cim * bim
    bbim_ref[...] = cre * bim + cim * bre
    pre_ref[...] = are
    pim_ref[...] = aim


def _ssm_prep(lam_re, lam_im, log_dt, b_re, b_im):
    n = SSM_LANES
    flat = lambda t: t.reshape(1, n)
    to_lanes = lambda t: t.reshape(n, SSM_GROUP).T
    ldt = jnp.broadcast_to(log_dt[:, None], (SSM_GROUPS, SSM_STATE))
    outs = pl.pallas_call(
        _ssm_prep_kernel,
        out_shape=[jax.ShapeDtypeStruct((1, n), F32), jax.ShapeDtypeStruct((1, n), F32),
                   jax.ShapeDtypeStruct((SSM_GROUP, n), F32), jax.ShapeDtypeStruct((SSM_GROUP, n), F32)],
    )(flat(lam_re), flat(lam_im), flat(ldt), to_lanes(b_re), to_lanes(b_im))
    return outs


def _group_mask_in():
    r = jnp.arange(SSM_WIDTH)[:, None] // SSM_GROUP
    c = jnp.arange(SSM_LANES)[None, :] // SSM_STATE
    return r == c


def _block_diag_in(bb):
    return jnp.where(_group_mask_in(), jnp.tile(bb, (SSM_GROUPS, 1)), 0.0).astype(BF16)


def _block_diag_out(cmat):
    per_lane = cmat.transpose(0, 2, 1).reshape(SSM_LANES, SSM_GROUP)
    return jnp.where(_group_mask_in().T, jnp.tile(per_lane, (1, SSM_GROUPS)), 0.0).astype(BF16)


def _permute_rows_f32(perm, x):
    hi = x.astype(BF16)
    rest = x - hi.astype(F32)
    mid = rest.astype(BF16)
    lo = (rest - mid.astype(F32)).astype(BF16)
    return (jnp.dot(perm, hi, preferred_element_type=F32) + jnp.dot(perm, mid, preferred_element_type=F32)
            + jnp.dot(perm, lo, preferred_element_type=F32))


def _ssm_kernel(u_ref, gb_ref, wbre_ref, wbim_ref, pre_ref, pim_ref, wcre_ref, wcim_ref,
                d_ref, wg_ref, bg_ref, o_ref, hre_ref, him_ref, sre_ref, sim_ref):
    bsz, tc, sw = u_ref.shape
    rows = bsz * tc

    @pl.when(pl.program_id(0) == 0)
    def _():
        sre_ref[...] = jnp.zeros_like(sre_ref)
        sim_ref[...] = jnp.zeros_like(sim_ref)

    u = u_ref[...].reshape(rows, sw)
    r_idx = lax.broadcasted_iota(jnp.int32, (rows, rows), 0)
    c_idx = lax.broadcasted_iota(jnp.int32, (rows, rows), 1)
    to_time_major = (c_idx == (r_idx % bsz) * tc + r_idx // bsz).astype(BF16)
    to_batch_major = (c_idx == (r_idx % tc) * bsz + r_idx // tc).astype(BF16)
    ub = jnp.dot(to_time_major, u.astype(BF16), preferred_element_type=F32).astype(BF16)
    hw, hn = SSM_WIDTH // 2, SSM_LANES // 2
    for half in range(2):
        chans, lns = slice(half * hw, (half + 1) * hw), slice(half * hn, (half + 1) * hn)
        hre_ref[:, lns] = jnp.dot(ub[:, chans], wbre_ref[chans, lns], preferred_element_type=F32)
        him_ref[:, lns] = jnp.dot(ub[:, chans], wbim_ref[chans, lns], preferred_element_type=F32)

    width = SSM_SCAN_TILES * LANES
    for j0 in range(0, SSM_LANES, width):
        ls = slice(j0, j0 + width)
        are = jnp.broadcast_to(pre_ref[:, ls], (bsz, width))
        aim = jnp.broadcast_to(pim_ref[:, ls], (bsz, width))

        def tick(t, h):
            hr, hi = h
            r0 = pl.multiple_of(t * bsz, bsz)
            nr = (are * hr - aim * hi) + hre_ref[pl.ds(r0, bsz), ls]
            ni = (are * hi + aim * hr) + him_ref[pl.ds(r0, bsz), ls]
            hre_ref[pl.ds(r0, bsz), ls] = nr
            him_ref[pl.ds(r0, bsz), ls] = ni
            return nr, ni

        hr, hi = lax.fori_loop(0, tc, tick, (sre_ref[:, ls], sim_ref[:, ls]), unroll=SSM_SCAN_UNROLL)
        sre_ref[:, ls] = hr
        sim_ref[:, ls] = hi

    ys = []
    for half in range(2):
        cols, lns = slice(half * hw, (half + 1) * hw), slice(half * hn, (half + 1) * hn)
        ys.append(jnp.dot(hre_ref[:, lns].astype(BF16), wcre_ref[lns, cols], preferred_element_type=F32)
                  - jnp.dot(him_ref[:, lns].astype(BF16), wcim_ref[lns, cols], preferred_element_type=F32))
    y = _permute_rows_f32(to_batch_major, jnp.concatenate(ys, axis=1))
    y = y + d_ref[...] * u
    y = jax.nn.gelu(y)
    y = y * jax.nn.sigmoid(_bdot(y, wg_ref[...]) + bg_ref[...])
    o_ref[...] = (y * gb_ref[...].reshape(rows, sw)).reshape(bsz, tc, sw)


def _ssm_branch(u, gb, wbre, wbim, pre, pim, wcre, wcim, d_skip, w_glu, b_glu):
    bsz, seq, sw = u.shape
    assert bsz == SUBLANES, "the scan keeps the batch on the sublanes of one vreg tile"
    tc = SSM_CHUNK
    n = SSM_LANES
    row = pl.BlockSpec((bsz, tc, sw), lambda i: (0, i, 0))
    full = lambda r, c: pl.BlockSpec((r, c), lambda i: (0, 0))
    return pl.pallas_call(
        _ssm_kernel,
        grid=(seq // tc,),
        in_specs=[row, row, full(sw, n), full(sw, n), full(1, n), full(1, n),
                  full(n, sw), full(n, sw), full(1, sw), full(sw, sw), full(1, sw)],
        out_specs=row,
        out_shape=jax.ShapeDtypeStruct((bsz, seq, sw), F32),
        scratch_shapes=[pltpu.VMEM((bsz * tc, n), F32), pltpu.VMEM((bsz * tc, n), F32),
                        pltpu.VMEM((bsz, n), F32), pltpu.VMEM((bsz, n), F32)],
        compiler_params=pltpu.CompilerParams(
            dimension_semantics=("arbitrary",), vmem_limit_bytes=VMEM_LIMIT_BYTES),
    )(u, gb, wbre, wbim, pre, pim, wcre, wcim, d_skip.reshape(1, sw), w_glu.astype(BF16),
      b_glu.reshape(1, sw))


def _mid_kernel(x_ref, ya_ref, yb_ref, mod0_ref, mod1_ref, g_ref, wo_ref, wi_ref,
                x1_ref, q_ref, k_ref, v_ref, sg_ref):
    cw, aw = CONV_WIDTH, ATTN_WIDTH
    y = (jnp.dot(ya_ref[0].astype(BF16), wo_ref[0:cw, :], preferred_element_type=F32)
         + jnp.dot(yb_ref[0].astype(BF16), wo_ref[cw:, :], preferred_element_type=F32))
    x1 = x_ref[0] + mod0_ref[0][2:3] * y
    x1_ref[0] = x1
    mod1 = mod1_ref[0]
    h = _rms_modulate(x1, g_ref[...], mod1[1:2], mod1[0:1]).astype(BF16)
    scale = ATTN_HEAD_DIM ** -0.5 * LOG2_E
    q_ref[0] = (jnp.dot(h, wi_ref[:, 0:aw], preferred_element_type=F32) * scale).astype(BF16)
    k_ref[0] = jnp.dot(h, wi_ref[:, aw:2 * aw], preferred_element_type=F32).astype(BF16)
    v_ref[0] = jnp.dot(h, wi_ref[:, 2 * aw:3 * aw], preferred_element_type=F32).astype(BF16)
    sg_ref[0] = _silu(jnp.dot(h, wi_ref[:, 3 * aw:4 * aw], preferred_element_type=F32))


def _mid(x, ya, yb, mod0, mod1, norm_g, w_out, w_in):
    bsz, seq, d = x.shape
    tl = ROW_TILE
    row = lambda width: pl.BlockSpec((1, tl, width), lambda b, i: (b, i, 0))
    modspec = pl.BlockSpec((1, 3, d), lambda b, i: (b, 0, 0))
    full = lambda r, c: pl.BlockSpec((r, c), lambda b, i: (0, 0))
    aw = ATTN_WIDTH
    return pl.pallas_call(
        _mid_kernel,
        grid=(bsz, seq // tl),
        in_specs=[row(d), row(CONV_WIDTH), row(SSM_WIDTH), modspec, modspec, full(1, d),
                  full(CONV_WIDTH + SSM_WIDTH, d), full(d, 4 * aw)],
        out_specs=[row(d), row(aw), row(aw), row(aw), row(aw)],
        out_shape=[jax.ShapeDtypeStruct((bsz, seq, d), F32),
                   jax.ShapeDtypeStruct((bsz, seq, aw), BF16),
                   jax.ShapeDtypeStruct((bsz, seq, aw), BF16),
                   jax.ShapeDtypeStruct((bsz, seq, aw), BF16),
                   jax.ShapeDtypeStruct((bsz, seq, aw), F32)],
        compiler_params=pltpu.CompilerParams(
            dimension_semantics=("parallel", "parallel"), vmem_limit_bytes=VMEM_LIMIT_BYTES),
    )(x, ya, yb, mod0, mod1, norm_g.reshape(1, d), w_out.astype(BF16), w_in.astype(BF16))


def _attn_schedule(nq):
    visits = []
    for q in range(nq):
        visits.append((q, q, False))
        visits.append((q, q - 1, False) if q >= 1 else (0, 0, True))
    for q in range(2, nq):
        visits.extend((q, j, False) for j in range(q - 2, -1, -1))
    visits.append((0, 0, True))
    if len(visits) % 2:
        visits.append((0, 0, True))
    rows = []
    prev = (0, 0, True)
    for q, j, dummy in visits:
        pq, pj, pdummy = prev
        rows.append((q, j, nq if pdummy else pq, pj, int(pdummy)))
        prev = (q, j, dummy)
    return np.asarray(rows, np.int32).T.copy(), 2 * nq, len(visits)


def _attn_kernel(tbl_ref, q_ref, k_ref, v_ref, o_ref, acc_ref, carry_ref, qn_ref, qp_ref,
                 sp0_ref, sp1_ref, e0_ref, e1_ref, *, diag_steps, total_steps):
    blk = ATTN_BLOCK
    pair = 2 * ATTN_HEAD_DIM
    heads = range(2 * ATTN_PAIRS)
    cols = [slice((s // 2) * pair, (s // 2 + 1) * pair) for s in heads]
    nq = q_ref.shape[1] // blk
    lane = lax.broadcasted_iota(jnp.int32, (blk, pair), 1)
    first_head = lane < ATTN_HEAD_DIM
    r_idx = lax.broadcasted_iota(jnp.int32, (blk, blk), 0)
    c_idx = lax.broadcasted_iota(jnp.int32, (blk, blk), 1)
    not_before = (r_idx >= c_idx).astype(BF16)
    causal = c_idx < r_idx
    nt = (((1,), (1,)), ((), ()))

    for qb in range(nq):
        for s in heads:
            q = q_ref[0, qb * blk:(qb + 1) * blk, cols[s]].astype(F32)
            own = first_head == (s % 2 == 0)
            qn_ref[qb, s] = jnp.where(own, -q, 0.0).astype(BF16)
            qp_ref[qb, s] = jnp.where(own, q, 0.0).astype(BF16)
    acc_ref[...] = jnp.zeros_like(acc_ref)
    carry_ref[...] = jnp.zeros_like(carry_ref)
    sp1_ref[...] = jnp.zeros_like(sp1_ref)
    e1_ref[...] = jnp.zeros_like(e1_ref)

    def step(i, diagonal, sp_a, e_a, sp_b, e_b):
        a_q, a_key = tbl_ref[0, i], tbl_ref[1, i]
        b_slot, b_key, b_dummy = tbl_ref[2, i], tbl_ref[3, i], tbl_ref[4, i]
        a_start = pl.multiple_of(a_key * blk, blk)
        b_start = pl.multiple_of(b_key * blk, blk)
        penalty = jnp.where(b_dummy == 1, -MASKED_LOGIT, 0.0)
        zs = [(lax.dot_general(qn_ref[a_q, s], k_ref[0, pl.ds(a_start, blk), cols[s]], nt,
                               preferred_element_type=F32),
               lax.dot_general(qp_ref[a_q, s], k_ref[0, pl.ds(a_start, blk), cols[s]], nt,
                               preferred_element_type=F32)) for s in heads]
        sums = [jnp.dot(sp_b[s], not_before, preferred_element_type=F32) for s in heads]
        for s in heads:
            zn, zp = zs[s]
            l = jnp.minimum(zn, 0.0) - jnp.log(1.0 + jnp.exp2(jnp.minimum(zn, zp))) * LOG2_E
            if diagonal:
                l = jnp.where(causal, l, 0.0)
                zn = jnp.where(causal, zn, MASKED_LOGIT)
            sp_a[s] = l.astype(BF16)
            e_a[s] = zn
        for s in heads:
            w = jnp.exp2((sums[s] + (carry_ref[b_slot, s] + penalty)) - e_b[s])
            acc_ref[b_slot, s] += jnp.dot(w.astype(BF16), v_ref[0, pl.ds(b_start, blk), cols[s]],
                                          preferred_element_type=F32)
            carry_ref[b_slot, s] += sums[s][:, 0:1]

    def diag_pair(p, _):
        step(2 * p, True, sp0_ref, e0_ref, sp1_ref, e1_ref)
        step(2 * p + 1, False, sp1_ref, e1_ref, sp0_ref, e0_ref)
        return 0

    def plain_pair(p, _):
        step(2 * p, False, sp0_ref, e0_ref, sp1_ref, e1_ref)
        step(2 * p + 1, False, sp1_ref, e1_ref, sp0_ref, e0_ref)
        return 0

    lax.fori_loop(0, diag_steps // 2, diag_pair, 0)
    lax.fori_loop(diag_steps // 2, total_steps // 2, plain_pair, 0)
    for qb in range(nq):
        for p in range(ATTN_PAIRS):
            o_ref[0, qb * blk:(qb + 1) * blk, cols[2 * p]] = jnp.where(
                first_head, acc_ref[qb, 2 * p], acc_ref[qb, 2 * p + 1])


def _attention(q, k, v):
    bsz, seq, aw = q.shape
    pair = 2 * ATTN_HEAD_DIM
    nh = 2 * ATTN_PAIRS
    width = ATTN_PAIRS * pair
    blk = ATTN_BLOCK
    nq = seq // blk
    table, diag_steps, total_steps = _attn_schedule(nq)
    assert total_steps % 2 == 0 and table.shape[1] == total_steps
    seq_block = pl.BlockSpec((1, seq, width), lambda b, h, tbl: (b, 0, h))
    return pl.pallas_call(
        functools.partial(_attn_kernel, diag_steps=diag_steps, total_steps=total_steps),
        grid_spec=pltpu.PrefetchScalarGridSpec(
            num_scalar_prefetch=1,
            grid=(bsz, aw // width),
            in_specs=[seq_block, seq_block, seq_block],
            out_specs=seq_block,
            scratch_shapes=[pltpu.VMEM((nq + 1, nh, blk, pair), F32),
                            pltpu.VMEM((nq + 1, nh, blk, 1), F32),
                            pltpu.VMEM((nq, nh, blk, pair), BF16),
                            pltpu.VMEM((nq, nh, blk, pair), BF16),
                            pltpu.VMEM((nh, blk, blk), BF16), pltpu.VMEM((nh, blk, blk), BF16),
                            pltpu.VMEM((nh, blk, blk), F32), pltpu.VMEM((nh, blk, blk), F32)]),
        out_shape=jax.ShapeDtypeStruct((bsz, seq, aw), F32),
        compiler_params=pltpu.CompilerParams(
            dimension_semantics=("parallel", "parallel"), vmem_limit_bytes=VMEM_LIMIT_BYTES),
    )(jnp.asarray(table), q, k, v)


def _final_kernel(x_ref, o_ref, sg_ref, mod_ref, wo_ref, g_ref, out_ref):
    y = _bdot(o_ref[0] * sg_ref[0], wo_ref[...])
    x2 = x_ref[0] + mod_ref[0][2:3] * y
    ms = jnp.mean(x2 * x2, axis=-1, keepdims=True)
    out_ref[0] = x2 * lax.rsqrt(ms + EPS) * g_ref[...]


def _final(x1, o, sg, mod1, w_out, final_g):
    bsz, seq, d = x1.shape
    tl = ROW_TILE
    row = pl.BlockSpec((1, tl, d), lambda b, i: (b, i, 0))
    return pl.pallas_call(
        _final_kernel,
        grid=(bsz, seq // tl),
        in_specs=[row, row, row,
                  pl.BlockSpec((1, 3, d), lambda b, i: (b, 0, 0)),
                  pl.BlockSpec((ATTN_WIDTH, d), lambda b, i: (0, 0)),
                  pl.BlockSpec((1, d), lambda b, i: (0, 0))],
        out_specs=row,
        out_shape=jax.ShapeDtypeStruct((bsz, seq, d), F32),
        compiler_params=pltpu.CompilerParams(dimension_semantics=("parallel", "parallel")),
    )(x1, o, sg, mod1, w_out.astype(BF16), final_g.reshape(1, d))


def kernel(x, c, l0_norm_g, l0_w_ada, l0_b_ada, l0_w_in, l0_conv_w, l0_conv_b, l0_conv_ln_g, l0_conv_ln_b, l0_ssm_lam_re, l0_ssm_lam_im, l0_ssm_log_dt, l0_ssm_b_re, l0_ssm_b_im, l0_ssm_c_re, l0_ssm_c_im, l0_ssm_d, l0_ssm_w_glu, l0_ssm_b_glu, l0_w_out, l1_norm_g, l1_w_ada, l1_b_ada, l1_w_in, l1_w_out, final_norm_g):
    bsz, seq, d = x.shape
    mod0 = _ada(c, l0_w_ada, l0_b_ada).reshape(bsz, 3, d)
    mod1 = _ada(c, l1_w_ada, l1_b_ada).reshape(bsz, 3, d)

    hg, ga, u, gb = _even_in(x, mod0, l0_norm_g, l0_w_in)
    ya = _conv_branch(hg, ga, l0_conv_w, l0_conv_b, l0_conv_ln_g, l0_conv_ln_b)

    pre, pim, bbre, bbim = _ssm_prep(l0_ssm_lam_re, l0_ssm_lam_im, l0_ssm_log_dt, l0_ssm_b_re, l0_ssm_b_im)
    yb = _ssm_branch(u, gb, _block_diag_in(bbre), _block_diag_in(bbim), pre, pim,
                     _block_diag_out(l0_ssm_c_re), _block_diag_out(l0_ssm_c_im),
                     l0_ssm_d, l0_ssm_w_glu, l0_ssm_b_glu)

    x1, q, k, v, sg = _mid(x, ya, yb, mod0, mod1, l1_norm_g, l0_w_out, l1_w_in)
    o = _attention(q, k, v)
    return _final(x1, o, sg, mod1, l1_w_out, final_norm_g)
```

```python
import functools
import math

import jax
import jax.numpy as jnp
from jax import lax
import numpy as np
from jax.experimental import pallas as pl
from jax.experimental.pallas import tpu as pltpu

F32 = jnp.float32
BF16 = jnp.bfloat16

D_MODEL = 1024
CONV_WIDTH = 1024
CONV_KERNEL = 31
SSM_WIDTH = 512
SSM_GROUP = 16
SSM_GROUPS = SSM_WIDTH // SSM_GROUP
SSM_STATE = 64
SSM_LANES = SSM_GROUPS * SSM_STATE
ATTN_HEADS = 16
ATTN_HEAD_DIM = 64
ATTN_WIDTH = ATTN_HEADS * ATTN_HEAD_DIM
EPS = 1e-6
LOG2_E = math.log2(math.e)
MASKED_LOGIT = 1e30

SUBLANES = 8
LANES = 128
VMEM_LIMIT_BYTES = 56 * 1024 * 1024

ROW_TILE = 512
CONV_TILE = 256
CONV_HALO = 32
CONV_CHUNK = 32
SSM_CHUNK = 32
SSM_SCAN_TILES = 8
SSM_SCAN_UNROLL = 4
ATTN_BLOCK = 256
ATTN_PAIRS = 2


def _silu(x):
    return x * jax.nn.sigmoid(x)


def _bdot(a, b):
    return jnp.dot(a.astype(BF16), b.astype(BF16), preferred_element_type=F32)


def _rms_modulate(x, g, scale, shift):
    ms = jnp.mean(x * x, axis=-1, keepdims=True)
    return (x * lax.rsqrt(ms + EPS) * g) * (1.0 + scale) + shift


def _ada_kernel(c_ref, w_ref, b_ref, o_ref):
    o_ref[...] = _bdot(_silu(c_ref[...]), w_ref[...]) + b_ref[...]


def _ada(c, w_ada, b_ada):
    bsz, d = c.shape
    n = w_ada.shape[1]
    tn = d
    return pl.pallas_call(
        _ada_kernel,
        grid=(n // tn,),
        in_specs=[pl.BlockSpec((bsz, d), lambda j: (0, 0)),
                  pl.BlockSpec((d, tn), lambda j: (0, j)),
                  pl.BlockSpec((1, tn), lambda j: (0, j))],
        out_specs=pl.BlockSpec((bsz, tn), lambda j: (0, j)),
        out_shape=jax.ShapeDtypeStruct((bsz, n), F32),
        compiler_params=pltpu.CompilerParams(dimension_semantics=("parallel",)),
    )(c, w_ada, b_ada.reshape(1, n))


def _even_in_kernel(x_ref, mod_ref, g_ref, w_ref, hg_ref, ga_ref, u_ref, gb_ref):
    mod = mod_ref[0]
    h = _rms_modulate(x_ref[0], g_ref[...], mod[1:2], mod[0:1]).astype(BF16)
    cw, sw = CONV_WIDTH, SSM_WIDTH
    val = jnp.dot(h, w_ref[:, 0:cw], preferred_element_type=F32)
    glu = jnp.dot(h, w_ref[:, cw:2 * cw], preferred_element_type=F32)
    hg_ref[0] = val * jax.nn.sigmoid(glu)
    ga_ref[0] = _silu(jnp.dot(h, w_ref[:, 2 * cw:3 * cw], preferred_element_type=F32))
    u_ref[0] = jnp.dot(h, w_ref[:, 3 * cw:3 * cw + sw], preferred_element_type=F32)
    gb_ref[0] = _silu(jnp.dot(h, w_ref[:, 3 * cw + sw:3 * cw + 2 * sw], preferred_element_type=F32))


def _even_in(x, mod, norm_g, w_in):
    bsz, seq, d = x.shape
    tl = ROW_TILE
    n_in = w_in.shape[1]
    row = lambda width: pl.BlockSpec((1, tl, width), lambda b, i: (b, i, 0))
    return pl.pallas_call(
        _even_in_kernel,
        grid=(bsz, seq // tl),
        in_specs=[row(d),
                  pl.BlockSpec((1, 3, d), lambda b, i: (b, 0, 0)),
                  pl.BlockSpec((1, d), lambda b, i: (0, 0)),
                  pl.BlockSpec((d, n_in), lambda b, i: (0, 0))],
        out_specs=[row(CONV_WIDTH), row(CONV_WIDTH), row(SSM_WIDTH), row(SSM_WIDTH)],
        out_shape=[jax.ShapeDtypeStruct((bsz, seq, CONV_WIDTH), F32),
                   jax.ShapeDtypeStruct((bsz, seq, CONV_WIDTH), F32),
                   jax.ShapeDtypeStruct((bsz, seq, SSM_WIDTH), F32),
                   jax.ShapeDtypeStruct((bsz, seq, SSM_WIDTH), F32)],
        compiler_params=pltpu.CompilerParams(
            dimension_semantics=("parallel", "parallel"), vmem_limit_bytes=VMEM_LIMIT_BYTES),
    )(x, mod, norm_g.reshape(1, d), w_in.astype(BF16))


def _conv_kernel(cur_ref, halo_ref, ga_ref, w_ref, cb_ref, lg_ref, lb_ref, o_ref, buf_ref, sh_ref):
    tl = cur_ref.shape[1]
    first = pl.program_id(1) == 0
    buf_ref[0:CONV_HALO, :] = jnp.where(first, 0.0, halo_ref[0])
    buf_ref[CONV_HALO:CONV_HALO + tl, :] = cur_ref[0]
    lead = CONV_HALO - (CONV_KERNEL - 1)
    rows = sh_ref.shape[1]
    for r in range(1, SUBLANES):
        sh_ref[r - 1] = buf_ref[r:r + rows, :]

    groups = CONV_CHUNK // SUBLANES
    for r0 in range(0, tl, CONV_CHUNK):
        acc = jnp.zeros((groups, SUBLANES, CONV_WIDTH), F32)
        for k in range(CONV_KERNEL):
            r, base = (lead + k) % SUBLANES, r0 + (lead + k) // SUBLANES * SUBLANES
            src = buf_ref if r == 0 else sh_ref.at[r - 1]
            rows8 = src[base:base + CONV_CHUNK, :].reshape(groups, SUBLANES, CONV_WIDTH)
            acc = acc + rows8 * w_ref[k * SUBLANES:(k + 1) * SUBLANES, :][None]
        acc = acc.reshape(CONV_CHUNK, CONV_WIDTH) + cb_ref[...]
        mu = jnp.mean(acc, axis=-1, keepdims=True)
        cen = acc - mu
        var = jnp.mean(cen * cen, axis=-1, keepdims=True)
        y = cen * lax.rsqrt(var + EPS) * lg_ref[...] + lb_ref[...]
        o_ref[0, r0:r0 + CONV_CHUNK, :] = _silu(y) * ga_ref[0, r0:r0 + CONV_CHUNK, :]


def _conv_branch(hg, ga, conv_w, conv_b, ln_g, ln_b):
    bsz, seq, cw = hg.shape
    tl = CONV_TILE
    halo_per_tile = tl // CONV_HALO
    row = pl.BlockSpec((1, tl, cw), lambda b, i: (b, i, 0))
    vec = pl.BlockSpec((1, cw), lambda b, i: (0, 0))
    return pl.pallas_call(
        _conv_kernel,
        grid=(bsz, seq // tl),
        in_specs=[row,
                  pl.BlockSpec((1, CONV_HALO, cw),
                               lambda b, i: (b, jnp.maximum(i * halo_per_tile - 1, 0), 0)),
                  row,
                  pl.BlockSpec((CONV_KERNEL * SUBLANES, cw), lambda b, i: (0, 0)),
                  vec, vec, vec],
        out_specs=row,
        out_shape=jax.ShapeDtypeStruct((bsz, seq, cw), F32),
        scratch_shapes=[pltpu.VMEM((CONV_HALO + tl, cw), F32),
                        pltpu.VMEM((SUBLANES - 1, CONV_HALO + tl - SUBLANES, cw), F32)],
        compiler_params=pltpu.CompilerParams(dimension_semantics=("parallel", "parallel")),
    )(hg, hg, ga, jnp.repeat(conv_w, SUBLANES, axis=0), conv_b.reshape(1, cw), ln_g.reshape(1, cw),
      ln_b.reshape(1, cw))


def _ssm_prep_kernel(lre_ref, lim_ref, ldt_ref, bre_ref, bim_ref, pre_ref, pim_ref, bbre_ref, bbim_ref):
    lre, lim = lre_ref[...], lim_ref[...]
    dt = jnp.exp(ldt_ref[...])
    mag = jnp.exp(lre * dt)
    are, aim = mag * jnp.cos(lim * dt), mag * jnp.sin(lim * dt)
    nre, nim = are - 1.0, aim
    den = lre * lre + lim * lim
    cre = (nre * lre + nim * lim) / den
    cim = (nim * lre - nre * lim) / den
    bre, bim = bre_ref[...], bim_ref[...]
    bbre_ref[...] = cre * bre - cim * bim
    bbim_ref[...] = cre * bim + cim * bre
    pre_ref[...] = are
    pim_ref[...] = aim


def _ssm_prep(lam_re, lam_im, log_dt, b_re, b_im):
    n = SSM_LANES
    flat = lambda t: t.reshape(1, n)
    to_lanes = lambda t: t.reshape(n, SSM_GROUP).T
    ldt = jnp.broadcast_to(log_dt[:, None], (SSM_GROUPS, SSM_STATE))
    outs = pl.pallas_call(
        _ssm_prep_kernel,
        out_shape=[jax.ShapeDtypeStruct((1, n), F32), jax.ShapeDtypeStruct((1, n), F32),
                   jax.ShapeDtypeStruct((SSM_GROUP, n), F32), jax.ShapeDtypeStruct((SSM_GROUP, n), F32)],
    )(flat(lam_re), flat(lam_im), flat(ldt), to_lanes(b_re), to_lanes(b_im))
    return outs


def _group_mask_in():
    r = jnp.arange(SSM_WIDTH)[:, None] // SSM_GROUP
    c = jnp.arange(SSM_LANES)[None, :] // SSM_STATE
    return r == c


def _block_diag_in(bb):
    return jnp.where(_group_mask_in(), jnp.tile(bb, (SSM_GROUPS, 1)), 0.0).astype(BF16)


def _block_diag_out(cmat):
    per_lane = cmat.transpose(0, 2, 1).reshape(SSM_LANES, SSM_GROUP)
    return jnp.where(_group_mask_in().T, jnp.tile(per_lane, (1, SSM_GROUPS)), 0.0).astype(BF16)


def _permute_rows_f32(perm, x):
    hi = x.astype(BF16)
    rest = x - hi.astype(F32)
    mid = rest.astype(BF16)
    lo = (rest - mid.astype(F32)).astype(BF16)
    return (jnp.dot(perm, hi, preferred_element_type=F32) + jnp.dot(perm, mid, preferred_element_type=F32)
            + jnp.dot(perm, lo, preferred_element_type=F32))


def _ssm_kernel(u_ref, gb_ref, wbre_ref, wbim_ref, pre_ref, pim_ref, wcre_ref, wcim_ref,
                d_ref, wg_ref, bg_ref, o_ref, hre_ref, him_ref, sre_ref, sim_ref):
    bsz, tc, sw = u_ref.shape
    rows = bsz * tc

    @pl.when(pl.program_id(0) == 0)
    def _():
        sre_ref[...] = jnp.zeros_like(sre_ref)
        sim_ref[...] = jnp.zeros_like(sim_ref)

    u = u_ref[...].reshape(rows, sw)
    r_idx = lax.broadcasted_iota(jnp.int32, (rows, rows), 0)
    c_idx = lax.broadcasted_iota(jnp.int32, (rows, rows), 1)
    to_time_major = (c_idx == (r_idx % bsz) * tc + r_idx // bsz).astype(BF16)
    to_batch_major = (c_idx == (r_idx % tc) * bsz + r_idx // tc).astype(BF16)
    ub = jnp.dot(to_time_major, u.astype(BF16), preferred_element_type=F32).astype(BF16)
    hw, hn = SSM_WIDTH // 2, SSM_LANES // 2
    for half in range(2):
        chans, lns = slice(half * hw, (half + 1) * hw), slice(half * hn, (half + 1) * hn)
        hre_ref[:, lns] = jnp.dot(ub[:, chans], wbre_ref[chans, lns], preferred_element_type=F32)
        him_ref[:, lns] = jnp.dot(ub[:, chans], wbim_ref[chans, lns], preferred_element_type=F32)

    width = SSM_SCAN_TILES * LANES
    for j0 in range(0, SSM_LANES, width):
        ls = slice(j0, j0 + width)
        are = jnp.broadcast_to(pre_ref[:, ls], (bsz, width))
        aim = jnp.broadcast_to(pim_ref[:, ls], (bsz, width))

        def tick(t, h):
            hr, hi = h
            r0 = pl.multiple_of(t * bsz, bsz)
            nr = (are * hr - aim * hi) + hre_ref[pl.ds(r0, bsz), ls]
            ni = (are * hi + aim * hr) + him_ref[pl.ds(r0, bsz), ls]
            hre_ref[pl.ds(r0, bsz), ls] = nr
            him_ref[pl.ds(r0, bsz), ls] = ni
            return nr, ni

        hr, hi = lax.fori_loop(0, tc, tick, (sre_ref[:, ls], sim_ref[:, ls]), unroll=SSM_SCAN_UNROLL)
        sre_ref[:, ls] = hr
        sim_ref[:, ls] = hi

    ys = []
    for half in range(2):
        cols, lns = slice(half * hw, (half + 1) * hw), slice(half * hn, (half + 1) * hn)
        ys.append(jnp.dot(hre_ref[:, lns].astype(BF16), wcre_ref[lns, cols], preferred_element_type=F32)
                  - jnp.dot(him_ref[:, lns].astype(BF16), wcim_ref[lns, cols], preferred_element_type=F32))
    y = _permute_rows_f32(to_batch_major, jnp.concatenate(ys, axis=1))
    y = y + d_ref[...] * u
    y = jax.nn.gelu(y)
    y = y * jax.nn.sigmoid(_bdot(y, wg_ref[...]) + bg_ref[...])
    o_ref[...] = (y * gb_ref[...].reshape(rows, sw)).reshape(bsz, tc, sw)


def _ssm_branch(u, gb, wbre, wbim, pre, pim, wcre, wcim, d_skip, w_glu, b_glu):
    bsz, seq, sw = u.shape
    assert bsz == SUBLANES, "the scan keeps the batch on the sublanes of one vreg tile"
    tc = SSM_CHUNK
    n = SSM_LANES
    row = pl.BlockSpec((bsz, tc, sw), lambda i: (0, i, 0))
    full = lambda r, c: pl.BlockSpec((r, c), lambda i: (0, 0))
    return pl.pallas_call(
        _ssm_kernel,
        grid=(seq // tc,),
        in_specs=[row, row, full(sw, n), full(sw, n), full(1, n), full(1, n),
                  full(n, sw), full(n, sw), full(1, sw), full(sw, sw), full(1, sw)],
        out_specs=row,
        out_shape=jax.ShapeDtypeStruct((bsz, seq, sw), F32),
        scratch_shapes=[pltpu.VMEM((bsz * tc, n), F32), pltpu.VMEM((bsz * tc, n), F32),
                        pltpu.VMEM((bsz, n), F32), pltpu.VMEM((bsz, n), F32)],
        compiler_params=pltpu.CompilerParams(
            dimension_semantics=("arbitrary",), vmem_limit_bytes=VMEM_LIMIT_BYTES),
    )(u, gb, wbre, wbim, pre, pim, wcre, wcim, d_skip.reshape(1, sw), w_glu.astype(BF16),
      b_glu.reshape(1, sw))


def _mid_kernel(x_ref, ya_ref, yb_ref, mod0_ref, mod1_ref, g_ref, wo_ref, wi_ref,
                x1_ref, q_ref, k_ref, v_ref, sg_ref):
    cw, aw = CONV_WIDTH, ATTN_WIDTH
    y = (jnp.dot(ya_ref[0].astype(BF16), wo_ref[0:cw, :], preferred_element_type=F32)
         + jnp.dot(yb_ref[0].astype(BF16), wo_ref[cw:, :], preferred_element_type=F32))
    x1 = x_ref[0] + mod0_ref[0][2:3] * y
    x1_ref[0] = x1
    mod1 = mod1_ref[0]
    h = _rms_modulate(x1, g_ref[...], mod1[1:2], mod1[0:1]).astype(BF16)
    scale = ATTN_HEAD_DIM ** -0.5 * LOG2_E
    q_ref[0] = (jnp.dot(h, wi_ref[:, 0:aw], preferred_element_type=F32) * scale).astype(BF16)
    k_ref[0] = jnp.dot(h, wi_ref[:, aw:2 * aw], preferred_element_type=F32).astype(BF16)
    v_ref[0] = jnp.dot(h, wi_ref[:, 2 * aw:3 * aw], preferred_element_type=F32).astype(BF16)
    sg_ref[0] = _silu(jnp.dot(h, wi_ref[:, 3 * aw:4 * aw], preferred_element_type=F32))


def _mid(x, ya, yb, mod0, mod1, norm_g, w_out, w_in):
    bsz, seq, d = x.shape
    tl = ROW_TILE
    row = lambda width: pl.BlockSpec((1, tl, width), lambda b, i: (b, i, 0))
    modspec = pl.BlockSpec((1, 3, d), lambda b, i: (b, 0, 0))
    full = lambda r, c: pl.BlockSpec((r, c), lambda b, i: (0, 0))
    aw = ATTN_WIDTH
    return pl.pallas_call(
        _mid_kernel,
        grid=(bsz, seq // tl),
        in_specs=[row(d), row(CONV_WIDTH), row(SSM_WIDTH), modspec, modspec, full(1, d),
                  full(CONV_WIDTH + SSM_WIDTH, d), full(d, 4 * aw)],
        out_specs=[row(d), row(aw), row(aw), row(aw), row(aw)],
        out_shape=[jax.ShapeDtypeStruct((bsz, seq, d), F32),
                   jax.ShapeDtypeStruct((bsz, seq, aw), BF16),
                   jax.ShapeDtypeStruct((bsz, seq, aw), BF16),
                   jax.ShapeDtypeStruct((bsz, seq, aw), BF16),
                   jax.ShapeDtypeStruct((bsz, seq, aw), F32)],
        compiler_params=pltpu.CompilerParams(
            dimension_semantics=("parallel", "parallel"), vmem_limit_bytes=VMEM_LIMIT_BYTES),
    )(x, ya, yb, mod0, mod1, norm_g.reshape(1, d), w_out.astype(BF16), w_in.astype(BF16))


def _attn_schedule(nq):
    visits = []
    for q in range(nq):
        visits.append((q, q, False))
        visits.append((q, q - 1, False) if q >= 1 else (0, 0, True))
    for q in range(2, nq):
        visits.extend((q, j, False) for j in range(q - 2, -1, -1))
    visits.append((0, 0, True))
    if len(visits) % 2:
        visits.append((0, 0, True))
    rows = []
    prev = (0, 0, True)
    for q, j, dummy in visits:
        pq, pj, pdummy = prev
        rows.append((q, j, nq if pdummy else pq, pj, int(pdummy)))
        prev = (q, j, dummy)
    return np.asarray(rows, np.int32).T.copy(), 2 * nq, len(visits)


def _attn_kernel(tbl_ref, q_ref, k_ref, v_ref, o_ref, acc_ref, carry_ref, qn_ref, qp_ref,
                 sp0_ref, sp1_ref, e0_ref, e1_ref, f0_ref, f1_ref, *, diag_steps, total_steps):
    blk = ATTN_BLOCK
    pair = 2 * ATTN_HEAD_DIM
    heads = range(2 * ATTN_PAIRS)
    cols = [slice((s // 2) * pair, (s // 2 + 1) * pair) for s in heads]
    nq = q_ref.shape[1] // blk
    lane = lax.broadcasted_iota(jnp.int32, (blk, pair), 1)
    first_head = lane < ATTN_HEAD_DIM
    r_idx = lax.broadcasted_iota(jnp.int32, (blk, blk), 0)
    c_idx = lax.broadcasted_iota(jnp.int32, (blk, blk), 1)
    after = (r_idx > c_idx).astype(BF16)
    causal = c_idx < r_idx
    nt = (((1,), (1,)), ((), ()))

    for qb in range(nq):
        for s in heads:
            q = q_ref[0, qb * blk:(qb + 1) * blk, cols[s]].astype(F32)
            own = first_head == (s % 2 == 0)
            qn_ref[qb, s] = jnp.where(own, -q, 0.0).astype(BF16)
            qp_ref[qb, s] = jnp.where(own, q, 0.0).astype(BF16)
    acc_ref[...] = jnp.zeros_like(acc_ref)
    carry_ref[...] = jnp.zeros_like(carry_ref)
    sp1_ref[...] = jnp.zeros_like(sp1_ref)
    e1_ref[...] = jnp.zeros_like(e1_ref)
    f1_ref[...] = jnp.zeros_like(f1_ref)

    def step(i, diagonal, sp_a, e_a, first_a, sp_b, e_b, first_b):
        a_q, a_key = tbl_ref[0, i], tbl_ref[1, i]
        b_slot, b_key, b_dummy = tbl_ref[2, i], tbl_ref[3, i], tbl_ref[4, i]
        a_start = pl.multiple_of(a_key * blk, blk)
        b_start = pl.multiple_of(b_key * blk, blk)
        penalty = jnp.where(b_dummy == 1, -MASKED_LOGIT, 0.0)
        zs = [(lax.dot_general(qn_ref[a_q, s], k_ref[0, pl.ds(a_start, blk), cols[s]], nt,
                               preferred_element_type=F32),
               lax.dot_general(qp_ref[a_q, s], k_ref[0, pl.ds(a_start, blk), cols[s]], nt,
                               preferred_element_type=F32)) for s in heads]
        sums = [jnp.dot(sp_b[s], after, preferred_element_type=F32) for s in heads]
        for s in heads:
            zn, zp = zs[s]
            l = jnp.minimum(zn, 0.0) - jnp.log(1.0 + jnp.exp2(jnp.minimum(zn, zp))) * LOG2_E
            neg_log_beta = zn - l
            if diagonal:
                l = jnp.where(causal, l, 0.0)
                neg_log_beta = jnp.where(causal, neg_log_beta, MASKED_LOGIT)
            sp_a[s] = l.astype(BF16)
            e_a[s] = neg_log_beta
            first_a[s] = l[:, 0:1]
        for s in heads:
            w = jnp.exp2((sums[s] + (carry_ref[b_slot, s] + penalty)) - e_b[s])
            acc_ref[b_slot, s] += jnp.dot(w.astype(BF16), v_ref[0, pl.ds(b_start, blk), cols[s]],
                                          preferred_element_type=F32)
            carry_ref[b_slot, s] += sums[s][:, 0:1] + first_b[s]

    def diag_pair(p, _):
        step(2 * p, True, sp0_ref, e0_ref, f0_ref, sp1_ref, e1_ref, f1_ref)
        step(2 * p + 1, False, sp1_ref, e1_ref, f1_ref, sp0_ref, e0_ref, f0_ref)
        return 0

    def plain_pair(p, _):
        step(2 * p, False, sp0_ref, e0_ref, f0_ref, sp1_ref, e1_ref, f1_ref)
        step(2 * p + 1, False, sp1_ref, e1_ref, f1_ref, sp0_ref, e0_ref, f0_ref)
        return 0

    lax.fori_loop(0, diag_steps // 2, diag_pair, 0)
    lax.fori_loop(diag_steps // 2, total_steps // 2, plain_pair, 0)
    for qb in range(nq):
        for p in range(ATTN_PAIRS):
            o_ref[0, qb * blk:(qb + 1) * blk, cols[2 * p]] = jnp.where(
                first_head, acc_ref[qb, 2 * p], acc_ref[qb, 2 * p + 1])


def _attention(q, k, v):
    bsz, seq, aw = q.shape
    pair = 2 * ATTN_HEAD_DIM
    nh = 2 * ATTN_PAIRS
    width = ATTN_PAIRS * pair
    blk = ATTN_BLOCK
    nq = seq // blk
    table, diag_steps, total_steps = _attn_schedule(nq)
    assert total_steps % 2 == 0 and table.shape[1] == total_steps
    seq_block = pl.BlockSpec((1, seq, width), lambda b, h, tbl: (b, 0, h))
    return pl.pallas_call(
        functools.partial(_attn_kernel, diag_steps=diag_steps, total_steps=total_steps),
        grid_spec=pltpu.PrefetchScalarGridSpec(
            num_scalar_prefetch=1,
            grid=(bsz, aw // width),
            in_specs=[seq_block, seq_block, seq_block],
            out_specs=seq_block,
            scratch_shapes=[pltpu.VMEM((nq + 1, nh, blk, pair), F32),
                            pltpu.VMEM((nq + 1, nh, blk, 1), F32),
                            pltpu.VMEM((nq, nh, blk, pair), BF16),
                            pltpu.VMEM((nq, nh, blk, pair), BF16),
                            pltpu.VMEM((nh, blk, blk), BF16), pltpu.VMEM((nh, blk, blk), BF16),
                            pltpu.VMEM((nh, blk, blk), F32), pltpu.VMEM((nh, blk, blk), F32),
                            pltpu.VMEM((nh, blk, 1), F32), pltpu.VMEM((nh, blk, 1), F32)]),
        out_shape=jax.ShapeDtypeStruct((bsz, seq, aw), F32),
        compiler_params=pltpu.CompilerParams(
            dimension_semantics=("parallel", "parallel"), vmem_limit_bytes=VMEM_LIMIT_BYTES),
    )(jnp.asarray(table), q, k, v)


def _final_kernel(x_ref, o_ref, sg_ref, mod_ref, wo_ref, g_ref, out_ref):
    y = _bdot(o_ref[0] * sg_ref[0], wo_ref[...])
    x2 = x_ref[0] + mod_ref[0][2:3] * y
    ms = jnp.mean(x2 * x2, axis=-1, keepdims=True)
    out_ref[0] = x2 * lax.rsqrt(ms + EPS) * g_ref[...]


def _final(x1, o, sg, mod1, w_out, final_g):
    bsz, seq, d = x1.shape
    tl = ROW_TILE
    row = pl.BlockSpec((1, tl, d), lambda b, i: (b, i, 0))
    return pl.pallas_call(
        _final_kernel,
        grid=(bsz, seq // tl),
        in_specs=[row, row, row,
                  pl.BlockSpec((1, 3, d), lambda b, i: (b, 0, 0)),
                  pl.BlockSpec((ATTN_WIDTH, d), lambda b, i: (0, 0)),
                  pl.BlockSpec((1, d), lambda b, i: (0, 0))],
        out_specs=row,
        out_shape=jax.ShapeDtypeStruct((bsz, seq, d), F32),
        compiler_params=pltpu.CompilerParams(dimension_semantics=("parallel", "parallel")),
    )(x1, o, sg, mod1, w_out.astype(BF16), final_g.reshape(1, d))


def kernel(x, c, l0_norm_g, l0_w_ada, l0_b_ada, l0_w_in, l0_conv_w, l0_conv_b, l0_conv_ln_g, l0_conv_ln_b, l0_ssm_lam_re, l0_ssm_lam_im, l0_ssm_log_dt, l0_ssm_b_re, l0_ssm_b_im, l0_ssm_c_re, l0_ssm_c_im, l0_ssm_d, l0_ssm_w_glu, l0_ssm_b_glu, l0_w_out, l1_norm_g, l1_w_ada, l1_b_ada, l1_w_in, l1_w_out, final_norm_g):
    bsz, seq, d = x.shape
    mod0 = _ada(c, l0_w_ada, l0_b_ada).reshape(bsz, 3, d)
    mod1 = _ada(c, l1_w_ada, l1_b_ada).reshape(bsz, 3, d)

    hg, ga, u, gb = _even_in(x, mod0, l0_norm_g, l0_w_in)
    ya = _conv_branch(hg, ga, l0_conv_w, l0_conv_b, l0_conv_ln_g, l0_conv_ln_b)

    pre, pim, bbre, bbim = _ssm_prep(l0_ssm_lam_re, l0_ssm_lam_im, l0_ssm_log_dt, l0_ssm_b_re, l0_ssm_b_im)
    yb = _ssm_branch(u, gb, _block_diag_in(bbre), _block_diag_in(bbim), pre, pim,
                     _block_diag_out(l0_ssm_c_re), _block_diag_out(l0_ssm_c_im),
                     l0_ssm_d, l0_ssm_w_glu, l0_ssm_b_glu)

    x1, q, k, v, sg = _mid(x, ya, yb, mod0, mod1, l1_norm_g, l0_w_out, l1_w_in)
    o = _attention(q, k, v)
    return _final(x1, o, sg, mod1, l1_w_out, final_norm_g)
```

```python
import functools
import math

import jax
import jax.numpy as jnp
from jax import lax
import numpy as np
from jax.experimental import pallas as pl
from jax.experimental.pallas import tpu as pltpu

F32 = jnp.float32
BF16 = jnp.bfloat16

D_MODEL = 1024
CONV_WIDTH = 1024
CONV_KERNEL = 31
SSM_WIDTH = 512
SSM_GROUP = 16
SSM_GROUPS = SSM_WIDTH // SSM_GROUP
SSM_STATE = 64
SSM_LANES = SSM_GROUPS * SSM_STATE
ATTN_HEADS = 16
ATTN_HEAD_DIM = 64
ATTN_WIDTH = ATTN_HEADS * ATTN_HEAD_DIM
EPS = 1e-6
LOG2_E = math.log2(math.e)
MASKED_LOGIT = 1e30

SUBLANES = 8
LANES = 128
VMEM_LIMIT_BYTES = 56 * 1024 * 1024

ROW_TILE = 512
CONV_TILE = 256
CONV_HALO = 32
CONV_CHUNK = 32
SSM_CHUNK = 32
SSM_SCAN_TILES = 8
SSM_SCAN_UNROLL = 4
ATTN_BLOCK = 256
ATTN_PAIRS = 2


def _silu(x):
    return x * jax.nn.sigmoid(x)


def _bdot(a, b):
    return jnp.dot(a.astype(BF16), b.astype(BF16), preferred_element_type=F32)


def _rms_modulate(x, g, scale, shift):
    ms = jnp.mean(x * x, axis=-1, keepdims=True)
    return (x * lax.rsqrt(ms + EPS) * g) * (1.0 + scale) + shift


def _ada_kernel(c_ref, w_ref, b_ref, o_ref):
    o_ref[...] = _bdot(_silu(c_ref[...]), w_ref[...]) + b_ref[...]


def _ada(c, w_ada, b_ada):
    bsz, d = c.shape
    n = w_ada.shape[1]
    tn = d
    return pl.pallas_call(
        _ada_kernel,
        grid=(n // tn,),
        in_specs=[pl.BlockSpec((bsz, d), lambda j: (0, 0)),
                  pl.BlockSpec((d, tn), lambda j: (0, j)),
                  pl.BlockSpec((1, tn), lambda j: (0, j))],
        out_specs=pl.BlockSpec((bsz, tn), lambda j: (0, j)),
        out_shape=jax.ShapeDtypeStruct((bsz, n), F32),
        compiler_params=pltpu.CompilerParams(dimension_semantics=("parallel",)),
    )(c, w_ada, b_ada.reshape(1, n))


def _even_in_kernel(x_ref, mod_ref, g_ref, w_ref, hg_ref, ga_ref, u_ref, gb_ref):
    mod = mod_ref[0]
    h = _rms_modulate(x_ref[0], g_ref[...], mod[1:2], mod[0:1]).astype(BF16)
    cw, sw = CONV_WIDTH, SSM_WIDTH
    val = jnp.dot(h, w_ref[:, 0:cw], preferred_element_type=F32)
    glu = jnp.dot(h, w_ref[:, cw:2 * cw], preferred_element_type=F32)
    hg_ref[0] = val * jax.nn.sigmoid(glu)
    ga_ref[0] = _silu(jnp.dot(h, w_ref[:, 2 * cw:3 * cw], preferred_element_type=F32))
    u_ref[0] = jnp.dot(h, w_ref[:, 3 * cw:3 * cw + sw], preferred_element_type=F32)
    gb_ref[0] = _silu(jnp.dot(h, w_ref[:, 3 * cw + sw:3 * cw + 2 * sw], preferred_element_type=F32))


def _even_in(x, mod, norm_g, w_in):
    bsz, seq, d = x.shape
    tl = ROW_TILE
    n_in = w_in.shape[1]
    row = lambda width: pl.BlockSpec((1, tl, width), lambda b, i: (b, i, 0))
    return pl.pallas_call(
        _even_in_kernel,
        grid=(bsz, seq // tl),
        in_specs=[row(d),
                  pl.BlockSpec((1, 3, d), lambda b, i: (b, 0, 0)),
                  pl.BlockSpec((1, d), lambda b, i: (0, 0)),
                  pl.BlockSpec((d, n_in), lambda b, i: (0, 0))],
        out_specs=[row(CONV_WIDTH), row(CONV_WIDTH), row(SSM_WIDTH), row(SSM_WIDTH)],
        out_shape=[jax.ShapeDtypeStruct((bsz, seq, CONV_WIDTH), F32),
                   jax.ShapeDtypeStruct((bsz, seq, CONV_WIDTH), F32),
                   jax.ShapeDtypeStruct((bsz, seq, SSM_WIDTH), F32),
                   jax.ShapeDtypeStruct((bsz, seq, SSM_WIDTH), F32)],
        compiler_params=pltpu.CompilerParams(
            dimension_semantics=("parallel", "parallel"), vmem_limit_bytes=VMEM_LIMIT_BYTES),
    )(x, mod, norm_g.reshape(1, d), w_in.astype(BF16))


def _conv_kernel(cur_ref, halo_ref, ga_ref, w_ref, cb_ref, lg_ref, lb_ref, o_ref, buf_ref, sh_ref):
    tl = cur_ref.shape[1]
    first = pl.program_id(1) == 0
    buf_ref[0:CONV_HALO, :] = jnp.where(first, 0.0, halo_ref[0])
    buf_ref[CONV_HALO:CONV_HALO + tl, :] = cur_ref[0]
    lead = CONV_HALO - (CONV_KERNEL - 1)
    rows = sh_ref.shape[1]
    for r in range(1, SUBLANES):
        sh_ref[r - 1] = buf_ref[r:r + rows, :]

    groups = CONV_CHUNK // SUBLANES
    for r0 in range(0, tl, CONV_CHUNK):
        acc = jnp.zeros((groups, SUBLANES, CONV_WIDTH), F32)
        for k in range(CONV_KERNEL):
            r, base = (lead + k) % SUBLANES, r0 + (lead + k) // SUBLANES * SUBLANES
            src = buf_ref if r == 0 else sh_ref.at[r - 1]
            rows8 = src[base:base + CONV_CHUNK, :].reshape(groups, SUBLANES, CONV_WIDTH)
            acc = acc + rows8 * w_ref[k * SUBLANES:(k + 1) * SUBLANES, :][None]
        acc = acc.reshape(CONV_CHUNK, CONV_WIDTH) + cb_ref[...]
        mu = jnp.mean(acc, axis=-1, keepdims=True)
        cen = acc - mu
        var = jnp.mean(cen * cen, axis=-1, keepdims=True)
        y = cen * lax.rsqrt(var + EPS) * lg_ref[...] + lb_ref[...]
        o_ref[0, r0:r0 + CONV_CHUNK, :] = _silu(y) * ga_ref[0, r0:r0 + CONV_CHUNK, :]


def _conv_branch(hg, ga, conv_w, conv_b, ln_g, ln_b):
    bsz, seq, cw = hg.shape
    tl = CONV_TILE
    halo_per_tile = tl // CONV_HALO
    row = pl.BlockSpec((1, tl, cw), lambda b, i: (b, i, 0))
    vec = pl.BlockSpec((1, cw), lambda b, i: (0, 0))
    return pl.pallas_call(
        _conv_kernel,
        grid=(bsz, seq // tl),
        in_specs=[row,
                  pl.BlockSpec((1, CONV_HALO, cw),
                               lambda b, i: (b, jnp.maximum(i * halo_per_tile - 1, 0), 0)),
                  row,
                  pl.BlockSpec((CONV_KERNEL * SUBLANES, cw), lambda b, i: (0, 0)),
                  vec, vec, vec],
        out_specs=row,
        out_shape=jax.ShapeDtypeStruct((bsz, seq, cw), F32),
        scratch_shapes=[pltpu.VMEM((CONV_HALO + tl, cw), F32),
                        pltpu.VMEM((SUBLANES - 1, CONV_HALO + tl - SUBLANES, cw), F32)],
        compiler_params=pltpu.CompilerParams(dimension_semantics=("parallel", "parallel")),
    )(hg, hg, ga, jnp.repeat(conv_w, SUBLANES, axis=0), conv_b.reshape(1, cw), ln_g.reshape(1, cw),
      ln_b.reshape(1, cw))


def _ssm_prep_kernel(lre_ref, lim_ref, ldt_ref, bre_ref, bim_ref, pre_ref, pim_ref, bbre_ref, bbim_ref):
    lre, lim = lre_ref[...], lim_ref[...]
    dt = jnp.exp(ldt_ref[...])
    mag = jnp.exp(lre * dt)
    are, aim = mag * jnp.cos(lim * dt), mag * jnp.sin(lim * dt)
    nre, nim = are - 1.0, aim
    den = lre * lre + lim * lim
    cre = (nre * lre + nim * lim) / den
    cim = (nim * lre - nre * lim) / den
    bre, bim = bre_ref[...], bim_ref[...]
    bbre_ref[...] = cre * bre - cim * bim
    bbim_ref[...] = cre * bim + cim * bre
    pre_ref[...] = are
    pim_ref[...] = aim


def _ssm_prep(lam_re, lam_im, log_dt, b_re, b_im):
    n = SSM_LANES
    flat = lambda t: t.reshape(1, n)
    to_lanes = lambda t: t.reshape(n, SSM_GROUP).T
    ldt = jnp.broadcast_to(log_dt[:, None], (SSM_GROUPS, SSM_STATE))
    outs = pl.pallas_call(
        _ssm_prep_kernel,
        out_shape=[jax.ShapeDtypeStruct((1, n), F32), jax.ShapeDtypeStruct((1, n), F32),
                   jax.ShapeDtypeStruct((SSM_GROUP, n), F32), jax.ShapeDtypeStruct((SSM_GROUP, n), F32)],
    )(flat(lam_re), flat(lam_im), flat(ldt), to_lanes(b_re), to_lanes(b_im))
    return outs


def _group_mask_in():
    r = jnp.arange(SSM_WIDTH)[:, None] // SSM_GROUP
    c = jnp.arange(SSM_LANES)[None, :] // SSM_STATE
    return r == c


def _block_diag_in(bb):
    return jnp.where(_group_mask_in(), jnp.tile(bb, (SSM_GROUPS, 1)), 0.0).astype(BF16)


def _block_diag_out(cmat):
    per_lane = cmat.transpose(0, 2, 1).reshape(SSM_LANES, SSM_GROUP)
    return jnp.where(_group_mask_in().T, jnp.tile(per_lane, (1, SSM_GROUPS)), 0.0).astype(BF16)


def _permute_rows_f32(perm, x):
    hi = x.astype(BF16)
    rest = x - hi.astype(F32)
    mid = rest.astype(BF16)
    lo = (rest - mid.astype(F32)).astype(BF16)
    return (jnp.dot(perm, hi, preferred_element_type=F32) + jnp.dot(perm, mid, preferred_element_type=F32)
            + jnp.dot(perm, lo, preferred_element_type=F32))


def _ssm_kernel(u_ref, gb_ref, wbre_ref, wbim_ref, pre_ref, pim_ref, wcre_ref, wcim_ref,
                d_ref, wg_ref, bg_ref, o_ref, hre_ref, him_ref, sre_ref, sim_ref):
    bsz, tc, sw = u_ref.shape
    rows = bsz * tc

    @pl.when(pl.program_id(0) == 0)
    def _():
        sre_ref[...] = jnp.zeros_like(sre_ref)
        sim_ref[...] = jnp.zeros_like(sim_ref)

    u = u_ref[...].reshape(rows, sw)
    r_idx = lax.broadcasted_iota(jnp.int32, (rows, rows), 0)
    c_idx = lax.broadcasted_iota(jnp.int32, (rows, rows), 1)
    to_time_major = (c_idx == (r_idx % bsz) * tc + r_idx // bsz).astype(BF16)
    to_batch_major = (c_idx == (r_idx % tc) * bsz + r_idx // tc).astype(BF16)
    ub = jnp.dot(to_time_major, u.astype(BF16), preferred_element_type=F32).astype(BF16)
    hw, hn = SSM_WIDTH // 2, SSM_LANES // 2
    for half in range(2):
        chans, lns = slice(half * hw, (half + 1) * hw), slice(half * hn, (half + 1) * hn)
        hre_ref[:, lns] = jnp.dot(ub[:, chans], wbre_ref[chans, lns], preferred_element_type=F32)
        him_ref[:, lns] = jnp.dot(ub[:, chans], wbim_ref[chans, lns], preferred_element_type=F32)

    width = SSM_SCAN_TILES * LANES
    for j0 in range(0, SSM_LANES, width):
        ls = slice(j0, j0 + width)
        are = jnp.broadcast_to(pre_ref[:, ls], (bsz, width))
        aim = jnp.broadcast_to(pim_ref[:, ls], (bsz, width))

        def tick(t, h):
            hr, hi = h
            r0 = pl.multiple_of(t * bsz, bsz)
            nr = (are * hr - aim * hi) + hre_ref[pl.ds(r0, bsz), ls]
            ni = (are * hi + aim * hr) + him_ref[pl.ds(r0, bsz), ls]
            hre_ref[pl.ds(r0, bsz), ls] = nr
            him_ref[pl.ds(r0, bsz), ls] = ni
            return nr, ni

        hr, hi = lax.fori_loop(0, tc, tick, (sre_ref[:, ls], sim_ref[:, ls]), unroll=SSM_SCAN_UNROLL)
        sre_ref[:, ls] = hr
        sim_ref[:, ls] = hi

    ys = []
    for half in range(2):
        cols, lns = slice(half * hw, (half + 1) * hw), slice(half * hn, (half + 1) * hn)
        ys.append(jnp.dot(hre_ref[:, lns].astype(BF16), wcre_ref[lns, cols], preferred_element_type=F32)
                  - jnp.dot(him_ref[:, lns].astype(BF16), wcim_ref[lns, cols], preferred_element_type=F32))
    y = _permute_rows_f32(to_batch_major, jnp.concatenate(ys, axis=1))
    y = y + d_ref[...] * u
    y = jax.nn.gelu(y)
    y = y * jax.nn.sigmoid(_bdot(y, wg_ref[...]) + bg_ref[...])
    o_ref[...] = (y * gb_ref[...].reshape(rows, sw)).reshape(bsz, tc, sw)


def _ssm_branch(u, gb, wbre, wbim, pre, pim, wcre, wcim, d_skip, w_glu, b_glu):
    bsz, seq, sw = u.shape
    assert bsz == SUBLANES, "the scan keeps the batch on the sublanes of one vreg tile"
    tc = SSM_CHUNK
    n = SSM_LANES
    row = pl.BlockSpec((bsz, tc, sw), lambda i: (0, i, 0))
    full = lambda r, c: pl.BlockSpec((r, c), lambda i: (0, 0))
    return pl.pallas_call(
        _ssm_kernel,
        grid=(seq // tc,),
        in_specs=[row, row, full(sw, n), full(sw, n), full(1, n), full(1, n),
                  full(n, sw), full(n, sw), full(1, sw), full(sw, sw), full(1, sw)],
        out_specs=row,
        out_shape=jax.ShapeDtypeStruct((bsz, seq, sw), F32),
        scratch_shapes=[pltpu.VMEM((bsz * tc, n), F32), pltpu.VMEM((bsz * tc, n), F32),
                        pltpu.VMEM((bsz, n), F32), pltpu.VMEM((bsz, n), F32)],
        compiler_params=pltpu.CompilerParams(
            dimension_semantics=("arbitrary",), vmem_limit_bytes=VMEM_LIMIT_BYTES),
    )(u, gb, wbre, wbim, pre, pim, wcre, wcim, d_skip.reshape(1, sw), w_glu.astype(BF16),
      b_glu.reshape(1, sw))


def _mid_kernel(x_ref, ya_ref, yb_ref, mod0_ref, mod1_ref, g_ref, wo_ref, wi_ref,
                x1_ref, q_ref, k_ref, v_ref, sg_ref):
    cw, aw = CONV_WIDTH, ATTN_WIDTH
    y = (jnp.dot(ya_ref[0].astype(BF16), wo_ref[0:cw, :], preferred_element_type=F32)
         + jnp.dot(yb_ref[0].astype(BF16), wo_ref[cw:, :], preferred_element_type=F32))
    x1 = x_ref[0] + mod0_ref[0][2:3] * y
    x1_ref[0] = x1
    mod1 = mod1_ref[0]
    h = _rms_modulate(x1, g_ref[...], mod1[1:2], mod1[0:1]).astype(BF16)
    scale = ATTN_HEAD_DIM ** -0.5 * LOG2_E
    q_ref[0] = (jnp.dot(h, wi_ref[:, 0:aw], preferred_element_type=F32) * scale).astype(BF16)
    k_ref[0] = jnp.dot(h, wi_ref[:, aw:2 * aw], preferred_element_type=F32).astype(BF16)
    v_ref[0] = jnp.dot(h, wi_ref[:, 2 * aw:3 * aw], preferred_element_type=F32).astype(BF16)
    sg_ref[0] = _silu(jnp.dot(h, wi_ref[:, 3 * aw:4 * aw], preferred_element_type=F32))


def _mid(x, ya, yb, mod0, mod1, norm_g, w_out, w_in):
    bsz, seq, d = x.shape
    tl = ROW_TILE
    row = lambda width: pl.BlockSpec((1, tl, width), lambda b, i: (b, i, 0))
    modspec = pl.BlockSpec((1, 3, d), lambda b, i: (b, 0, 0))
    full = lambda r, c: pl.BlockSpec((r, c), lambda b, i: (0, 0))
    aw = ATTN_WIDTH
    return pl.pallas_call(
        _mid_kernel,
        grid=(bsz, seq // tl),
        in_specs=[row(d), row(CONV_WIDTH), row(SSM_WIDTH), modspec, modspec, full(1, d),
                  full(CONV_WIDTH + SSM_WIDTH, d), full(d, 4 * aw)],
        out_specs=[row(d), row(aw), row(aw), row(aw), row(aw)],
        out_shape=[jax.ShapeDtypeStruct((bsz, seq, d), F32),
                   jax.ShapeDtypeStruct((bsz, seq, aw), BF16),
                   jax.ShapeDtypeStruct((bsz, seq, aw), BF16),
                   jax.ShapeDtypeStruct((bsz, seq, aw), BF16),
                   jax.ShapeDtypeStruct((bsz, seq, aw), F32)],
        compiler_params=pltpu.CompilerParams(
            dimension_semantics=("parallel", "parallel"), vmem_limit_bytes=VMEM_LIMIT_BYTES),
    )(x, ya, yb, mod0, mod1, norm_g.reshape(1, d), w_out.astype(BF16), w_in.astype(BF16))


def _attn_schedule(nq):
    visits = []
    for q in range(nq):
        visits.append((q, q, False))
        visits.append((q, q - 1, False) if q >= 1 else (0, 0, True))
    for q in range(2, nq):
        visits.extend((q, j, False) for j in range(q - 2, -1, -1))
    visits.append((0, 0, True))
    if len(visits) % 2:
        visits.append((0, 0, True))
    rows = []
    prev = (0, 0, True)
    for q, j, dummy in visits:
        pq, pj, pdummy = prev
        rows.append((q, j, nq if pdummy else pq, pj, int(pdummy)))
        prev = (q, j, dummy)
    return np.asarray(rows, np.int32).T.copy(), 2 * nq, len(visits)


def _attn_kernel(tbl_ref, q_ref, k_ref, v_ref, o_ref, acc_ref, carry_ref, qn_ref, qp_ref,
                 sp0_ref, sp1_ref, e0_ref, e1_ref, f0_ref, f1_ref, *, diag_steps, total_steps):
    blk = ATTN_BLOCK
    pair = 2 * ATTN_HEAD_DIM
    heads = range(2 * ATTN_PAIRS)
    cols = [slice((s // 2) * pair, (s // 2 + 1) * pair) for s in heads]
    nq = q_ref.shape[1] // blk
    lane = lax.broadcasted_iota(jnp.int32, (blk, pair), 1)
    first_head = lane < ATTN_HEAD_DIM
    r_idx = lax.broadcasted_iota(jnp.int32, (blk, blk), 0)
    c_idx = lax.broadcasted_iota(jnp.int32, (blk, blk), 1)
    after = (r_idx > c_idx).astype(BF16)
    causal = c_idx < r_idx
    nt = (((1,), (1,)), ((), ()))

    for qb in range(nq):
        for s in heads:
            q = q_ref[0, qb * blk:(qb + 1) * blk, cols[s]].astype(F32)
            own = first_head == (s % 2 == 0)
            qn_ref[qb, s] = jnp.where(own, -q, 0.0).astype(BF16)
            qp_ref[qb, s] = jnp.where(own, q, 0.0).astype(BF16)
    acc_ref[...] = jnp.zeros_like(acc_ref)
    carry_ref[...] = jnp.zeros_like(carry_ref)
    sp1_ref[...] = jnp.zeros_like(sp1_ref)
    e1_ref[...] = jnp.zeros_like(e1_ref)
    f1_ref[...] = jnp.zeros_like(f1_ref)

    def step(i, diagonal, sp_a, e_a, first_a, sp_b, e_b, first_b):
        a_q, a_key = tbl_ref[0, i], tbl_ref[1, i]
        b_slot, b_key, b_dummy = tbl_ref[2, i], tbl_ref[3, i], tbl_ref[4, i]
        a_start = pl.multiple_of(a_key * blk, blk)
        b_start = pl.multiple_of(b_key * blk, blk)
        penalty = jnp.where(b_dummy == 1, -MASKED_LOGIT, 0.0)
        sums = [jnp.dot(sp_b[s], after, preferred_element_type=F32) for s in heads]
        zs = [(lax.dot_general(qn_ref[a_q, s], k_ref[0, pl.ds(a_start, blk), cols[s]], nt,
                               preferred_element_type=F32),
               lax.dot_general(qp_ref[a_q, s], k_ref[0, pl.ds(a_start, blk), cols[s]], nt,
                               preferred_element_type=F32)) for s in heads]
        for s in heads:
            zn, zp = zs[s]
            l = jnp.minimum(zn, 0.0) - jnp.log(1.0 + jnp.exp2(jnp.minimum(zn, zp))) * LOG2_E
            neg_log_beta = zn - l
            if diagonal:
                l = jnp.where(causal, l, 0.0)
                neg_log_beta = jnp.where(causal, neg_log_beta, MASKED_LOGIT)
            sp_a[s] = l.astype(BF16)
            e_a[s] = neg_log_beta
            first_a[s] = l[:, 0:1]
        for s in heads:
            w = jnp.exp2((sums[s] + (carry_ref[b_slot, s] + penalty)) - e_b[s])
            acc_ref[b_slot, s] += jnp.dot(w.astype(BF16), v_ref[0, pl.ds(b_start, blk), cols[s]],
                                          preferred_element_type=F32)
            carry_ref[b_slot, s] += sums[s][:, 0:1] + first_b[s]

    def diag_pair(p, _):
        step(2 * p, True, sp0_ref, e0_ref, f0_ref, sp1_ref, e1_ref, f1_ref)
        step(2 * p + 1, False, sp1_ref, e1_ref, f1_ref, sp0_ref, e0_ref, f0_ref)
        return 0

    def plain_pair(p, _):
        step(2 * p, False, sp0_ref, e0_ref, f0_ref, sp1_ref, e1_ref, f1_ref)
        step(2 * p + 1, False, sp1_ref, e1_ref, f1_ref, sp0_ref, e0_ref, f0_ref)
        return 0

    lax.fori_loop(0, diag_steps // 2, diag_pair, 0)
    lax.fori_loop(diag_steps // 2, total_steps // 2, plain_pair, 0)
    for qb in range(nq):
        for p in range(ATTN_PAIRS):
            o_ref[0, qb * blk:(qb + 1) * blk, cols[2 * p]] = jnp.where(
                first_head, acc_ref[qb, 2 * p], acc_ref[qb, 2 * p + 1])


def _attention(q, k, v):
    bsz, seq, aw = q.shape
    pair = 2 * ATTN_HEAD_DIM
    nh = 2 * ATTN_PAIRS
    width = ATTN_PAIRS * pair
    blk = ATTN_BLOCK
    nq = seq // blk
    table, diag_steps, total_steps = _attn_schedule(nq)
    assert total_steps % 2 == 0 and table.shape[1] == total_steps
    seq_block = pl.BlockSpec((1, seq, width), lambda b, h, tbl: (b, 0, h))
    return pl.pallas_call(
        functools.partial(_attn_kernel, diag_steps=diag_steps, total_steps=total_steps),
        grid_spec=pltpu.PrefetchScalarGridSpec(
            num_scalar_prefetch=1,
            grid=(bsz, aw // width),
            in_specs=[seq_block, seq_block, seq_block],
            out_specs=seq_block,
            scratch_shapes=[pltpu.VMEM((nq + 1, nh, blk, pair), F32),
                            pltpu.VMEM((nq + 1, nh, blk, 1), F32),
                            pltpu.VMEM((nq, nh, blk, pair), BF16),
                            pltpu.VMEM((nq, nh, blk, pair), BF16),
                            pltpu.VMEM((nh, blk, blk), BF16), pltpu.VMEM((nh, blk, blk), BF16),
                            pltpu.VMEM((nh, blk, blk), F32), pltpu.VMEM((nh, blk, blk), F32),
                            pltpu.VMEM((nh, blk, 1), F32), pltpu.VMEM((nh, blk, 1), F32)]),
        out_shape=jax.ShapeDtypeStruct((bsz, seq, aw), F32),
        compiler_params=pltpu.CompilerParams(
            dimension_semantics=("parallel", "parallel"), vmem_limit_bytes=VMEM_LIMIT_BYTES),
    )(jnp.asarray(table), q, k, v)


def _final_kernel(x_ref, o_ref, sg_ref, mod_ref, wo_ref, g_ref, out_ref):
    y = _bdot(o_ref[0] * sg_ref[0], wo_ref[...])
    x2 = x_ref[0] + mod_ref[0][2:3] * y
    ms = jnp.mean(x2 * x2, axis=-1, keepdims=True)
    out_ref[0] = x2 * lax.rsqrt(ms + EPS) * g_ref[...]


def _final(x1, o, sg, mod1, w_out, final_g):
    bsz, seq, d = x1.shape
    tl = ROW_TILE
    row = pl.BlockSpec((1, tl, d), lambda b, i: (b, i, 0))
    return pl.pallas_call(
        _final_kernel,
        grid=(bsz, seq // tl),
        in_specs=[row, row, row,
                  pl.BlockSpec((1, 3, d), lambda b, i: (b, 0, 0)),
                  pl.BlockSpec((ATTN_WIDTH, d), lambda b, i: (0, 0)),
                  pl.BlockSpec((1, d), lambda b, i: (0, 0))],
        out_specs=row,
        out_shape=jax.ShapeDtypeStruct((bsz, seq, d), F32),
        compiler_params=pltpu.CompilerParams(dimension_semantics=("parallel", "parallel")),
    )(x1, o, sg, mod1, w_out.astype(BF16), final_g.reshape(1, d))


def kernel(x, c, l0_norm_g, l0_w_ada, l0_b_ada, l0_w_in, l0_conv_w, l0_conv_b, l0_conv_ln_g, l0_conv_ln_b, l0_ssm_lam_re, l0_ssm_lam_im, l0_ssm_log_dt, l0_ssm_b_re, l0_ssm_b_im, l0_ssm_c_re, l0_ssm_c_im, l0_ssm_d, l0_ssm_w_glu, l0_ssm_b_glu, l0_w_out, l1_norm_g, l1_w_ada, l1_b_ada, l1_w_in, l1_w_out, final_norm_g):
    bsz, seq, d = x.shape
    mod0 = _ada(c, l0_w_ada, l0_b_ada).reshape(bsz, 3, d)
    mod1 = _ada(c, l1_w_ada, l1_b_ada).reshape(bsz, 3, d)

    hg, ga, u, gb = _even_in(x, mod0, l0_norm_g, l0_w_in)
    ya = _conv_branch(hg, ga, l0_conv_w, l0_conv_b, l0_conv_ln_g, l0_conv_ln_b)

    pre, pim, bbre, bbim = _ssm_prep(l0_ssm_lam_re, l0_ssm_lam_im, l0_ssm_log_dt, l0_ssm_b_re, l0_ssm_b_im)
    yb = _ssm_branch(u, gb, _block_diag_in(bbre), _block_diag_in(bbim), pre, pim,
                     _block_diag_out(l0_ssm_c_re), _block_diag_out(l0_ssm_c_im),
                     l0_ssm_d, l0_ssm_w_glu, l0_ssm_b_glu)

    x1, q, k, v, sg = _mid(x, ya, yb, mod0, mod1, l1_norm_g, l0_w_out, l1_w_in)
    o = _attention(q, k, v)
    return _final(x1, o, sg, mod1, l1_w_out, final_norm_g)
```

```python
import functools
import math

import jax
import jax.numpy as jnp
from jax import lax
import numpy as np
from jax.experimental import pallas as pl
from jax.experimental.pallas import tpu as pltpu

F32 = jnp.float32
BF16 = jnp.bfloat16

D_MODEL = 1024
CONV_WIDTH = 1024
CONV_KERNEL = 31
SSM_WIDTH = 512
SSM_GROUP = 16
SSM_GROUPS = SSM_WIDTH // SSM_GROUP
SSM_STATE = 64
SSM_LANES = SSM_GROUPS * SSM_STATE
ATTN_HEADS = 16
ATTN_HEAD_DIM = 64
ATTN_WIDTH = ATTN_HEADS * ATTN_HEAD_DIM
EPS = 1e-6
LOG2_E = math.log2(math.e)
MASKED_LOGIT = 1e30

SUBLANES = 8
LANES = 128
VMEM_LIMIT_BYTES = 56 * 1024 * 1024

ROW_TILE = 512
CONV_TILE = 256
CONV_HALO = 32
CONV_CHUNK = 32
SSM_CHUNK = 32
SSM_SCAN_TILES = 8
SSM_SCAN_UNROLL = 4
ATTN_BLOCK = 256
ATTN_PAIRS = 2


def _silu(x):
    return x * jax.nn.sigmoid(x)


def _bdot(a, b):
    return jnp.dot(a.astype(BF16), b.astype(BF16), preferred_element_type=F32)


def _rms_modulate(x, g, scale, shift):
    ms = jnp.mean(x * x, axis=-1, keepdims=True)
    return (x * lax.rsqrt(ms + EPS) * g) * (1.0 + scale) + shift


def _ada_kernel(c_ref, w_ref, b_ref, o_ref):
    o_ref[...] = _bdot(_silu(c_ref[...]), w_ref[...]) + b_ref[...]


def _ada(c, w_ada, b_ada):
    bsz, d = c.shape
    n = w_ada.shape[1]
    tn = d
    return pl.pallas_call(
        _ada_kernel,
        grid=(n // tn,),
        in_specs=[pl.BlockSpec((bsz, d), lambda j: (0, 0)),
                  pl.BlockSpec((d, tn), lambda j: (0, j)),
                  pl.BlockSpec((1, tn), lambda j: (0, j))],
        out_specs=pl.BlockSpec((bsz, tn), lambda j: (0, j)),
        out_shape=jax.ShapeDtypeStruct((bsz, n), F32),
        compiler_params=pltpu.CompilerParams(dimension_semantics=("parallel",)),
    )(c, w_ada, b_ada.reshape(1, n))


def _even_in_kernel(x_ref, mod_ref, g_ref, w_ref, hg_ref, ga_ref, u_ref, gb_ref):
    mod = mod_ref[0]
    cw, sw = CONV_WIDTH, SSM_WIDTH
    tl = x_ref.shape[1]
    halves = [slice(r, r + tl // 2) for r in (0, tl // 2)]
    hs = [_rms_modulate(x_ref[0, rows, :], g_ref[...], mod[1:2], mod[0:1]).astype(BF16) for rows in halves]
    for rows, h in zip(halves, hs):
        val = jnp.dot(h, w_ref[:, 0:cw], preferred_element_type=F32)
        glu = jnp.dot(h, w_ref[:, cw:2 * cw], preferred_element_type=F32)
        hg_ref[0, rows, :] = val * jax.nn.sigmoid(glu)
        ga_ref[0, rows, :] = _silu(jnp.dot(h, w_ref[:, 2 * cw:3 * cw], preferred_element_type=F32))
        u_ref[0, rows, :] = jnp.dot(h, w_ref[:, 3 * cw:3 * cw + sw], preferred_element_type=F32)
        gb_ref[0, rows, :] = _silu(jnp.dot(h, w_ref[:, 3 * cw + sw:3 * cw + 2 * sw], preferred_element_type=F32))


def _even_in(x, mod, norm_g, w_in):
    bsz, seq, d = x.shape
    tl = ROW_TILE
    n_in = w_in.shape[1]
    row = lambda width: pl.BlockSpec((1, tl, width), lambda b, i: (b, i, 0))
    return pl.pallas_call(
        _even_in_kernel,
        grid=(bsz, seq // tl),
        in_specs=[row(d),
                  pl.BlockSpec((1, 3, d), lambda b, i: (b, 0, 0)),
                  pl.BlockSpec((1, d), lambda b, i: (0, 0)),
                  pl.BlockSpec((d, n_in), lambda b, i: (0, 0))],
        out_specs=[row(CONV_WIDTH), row(CONV_WIDTH), row(SSM_WIDTH), row(SSM_WIDTH)],
        out_shape=[jax.ShapeDtypeStruct((bsz, seq, CONV_WIDTH), F32),
                   jax.ShapeDtypeStruct((bsz, seq, CONV_WIDTH), F32),
                   jax.ShapeDtypeStruct((bsz, seq, SSM_WIDTH), F32),
                   jax.ShapeDtypeStruct((bsz, seq, SSM_WIDTH), F32)],
        compiler_params=pltpu.CompilerParams(
            dimension_semantics=("parallel", "parallel"), vmem_limit_bytes=VMEM_LIMIT_BYTES),
    )(x, mod, norm_g.reshape(1, d), w_in.astype(BF16))


def _conv_kernel(cur_ref, halo_ref, ga_ref, w_ref, cb_ref, lg_ref, lb_ref, o_ref, buf_ref, sh_ref):
    tl = cur_ref.shape[1]
    first = pl.program_id(1) == 0
    buf_ref[0:CONV_HALO, :] = jnp.where(first, 0.0, halo_ref[0])
    buf_ref[CONV_HALO:CONV_HALO + tl, :] = cur_ref[0]
    lead = CONV_HALO - (CONV_KERNEL - 1)
    rows = sh_ref.shape[1]
    for r in range(1, SUBLANES):
        sh_ref[r - 1] = buf_ref[r:r + rows, :]

    groups = CONV_CHUNK // SUBLANES
    for r0 in range(0, tl, CONV_CHUNK):
        acc = jnp.zeros((groups, SUBLANES, CONV_WIDTH), F32)
        for k in range(CONV_KERNEL):
            r, base = (lead + k) % SUBLANES, r0 + (lead + k) // SUBLANES * SUBLANES
            src = buf_ref if r == 0 else sh_ref.at[r - 1]
            rows8 = src[base:base + CONV_CHUNK, :].reshape(groups, SUBLANES, CONV_WIDTH)
            acc = acc + rows8 * w_ref[k * SUBLANES:(k + 1) * SUBLANES, :][None]
        acc = acc.reshape(CONV_CHUNK, CONV_WIDTH) + cb_ref[...]
        mu = jnp.mean(acc, axis=-1, keepdims=True)
        cen = acc - mu
        var = jnp.mean(cen * cen, axis=-1, keepdims=True)
        y = cen * lax.rsqrt(var + EPS) * lg_ref[...] + lb_ref[...]
        o_ref[0, r0:r0 + CONV_CHUNK, :] = _silu(y) * ga_ref[0, r0:r0 + CONV_CHUNK, :]


def _conv_branch(hg, ga, conv_w, conv_b, ln_g, ln_b):
    bsz, seq, cw = hg.shape
    tl = CONV_TILE
    halo_per_tile = tl // CONV_HALO
    row = pl.BlockSpec((1, tl, cw), lambda b, i: (b, i, 0))
    vec = pl.BlockSpec((1, cw), lambda b, i: (0, 0))
    return pl.pallas_call(
        _conv_kernel,
        grid=(bsz, seq // tl),
        in_specs=[row,
                  pl.BlockSpec((1, CONV_HALO, cw),
                               lambda b, i: (b, jnp.maximum(i * halo_per_tile - 1, 0), 0)),
                  row,
                  pl.BlockSpec((CONV_KERNEL * SUBLANES, cw), lambda b, i: (0, 0)),
                  vec, vec, vec],
        out_specs=row,
        out_shape=jax.ShapeDtypeStruct((bsz, seq, cw), F32),
        scratch_shapes=[pltpu.VMEM((CONV_HALO + tl, cw), F32),
                        pltpu.VMEM((SUBLANES - 1, CONV_HALO + tl - SUBLANES, cw), F32)],
        compiler_params=pltpu.CompilerParams(dimension_semantics=("parallel", "parallel")),
    )(hg, hg, ga, jnp.repeat(conv_w, SUBLANES, axis=0), conv_b.reshape(1, cw), ln_g.reshape(1, cw),
      ln_b.reshape(1, cw))


def _ssm_prep_kernel(lre_ref, lim_ref, ldt_ref, bre_ref, bim_ref, pre_ref, pim_ref, bbre_ref, bbim_ref):
    lre, lim = lre_ref[...], lim_ref[...]
    dt = jnp.exp(ldt_ref[...])
    mag = jnp.exp(lre * dt)
    are, aim = mag * jnp.cos(lim * dt), mag * jnp.sin(lim * dt)
    nre, nim = are - 1.0, aim
    den = lre * lre + lim * lim
    cre = (nre * lre + nim * lim) / den
    cim = (nim * lre - nre * lim) / den
    bre, bim = bre_ref[...], bim_ref[...]
    bbre_ref[...] = cre * bre - cim * bim
    bbim_ref[...] = cre * bim + cim * bre
    pre_ref[...] = are
    pim_ref[...] = aim


def _ssm_prep(lam_re, lam_im, log_dt, b_re, b_im):
    n = SSM_LANES
    flat = lambda t: t.reshape(1, n)
    to_lanes = lambda t: t.reshape(n, SSM_GROUP).T
    ldt = jnp.broadcast_to(log_dt[:, None], (SSM_GROUPS, SSM_STATE))
    outs = pl.pallas_call(
        _ssm_prep_kernel,
        out_shape=[jax.ShapeDtypeStruct((1, n), F32), jax.ShapeDtypeStruct((1, n), F32),
                   jax.ShapeDtypeStruct((SSM_GROUP, n), F32), jax.ShapeDtypeStruct((SSM_GROUP, n), F32)],
    )(flat(lam_re), flat(lam_im), flat(ldt), to_lanes(b_re), to_lanes(b_im))
    return outs


def _group_mask_in():
    r = jnp.arange(SSM_WIDTH)[:, None] // SSM_GROUP
    c = jnp.arange(SSM_LANES)[None, :] // SSM_STATE
    return r == c


def _block_diag_in(bb):
    return jnp.where(_group_mask_in(), jnp.tile(bb, (SSM_GROUPS, 1)), 0.0).astype(BF16)


def _block_diag_out(cmat):
    per_lane = cmat.transpose(0, 2, 1).reshape(SSM_LANES, SSM_GROUP)
    return jnp.where(_group_mask_in().T, jnp.tile(per_lane, (1, SSM_GROUPS)), 0.0).astype(BF16)


def _permute_rows_f32(perm, x):
    hi = x.astype(BF16)
    rest = x - hi.astype(F32)
    mid = rest.astype(BF16)
    lo = (rest - mid.astype(F32)).astype(BF16)
    return (jnp.dot(perm, hi, preferred_element_type=F32) + jnp.dot(perm, mid, preferred_element_type=F32)
            + jnp.dot(perm, lo, preferred_element_type=F32))


def _ssm_kernel(u_ref, gb_ref, wbre_ref, wbim_ref, pre_ref, pim_ref, wcre_ref, wcim_ref,
                d_ref, wg_ref, bg_ref, o_ref, hre_ref, him_ref, sre_ref, sim_ref):
    bsz, tc, sw = u_ref.shape
    rows = bsz * tc

    @pl.when(pl.program_id(0) == 0)
    def _():
        sre_ref[...] = jnp.zeros_like(sre_ref)
        sim_ref[...] = jnp.zeros_like(sim_ref)

    u = u_ref[...].reshape(rows, sw)
    r_idx = lax.broadcasted_iota(jnp.int32, (rows, rows), 0)
    c_idx = lax.broadcasted_iota(jnp.int32, (rows, rows), 1)
    to_time_major = (c_idx == (r_idx % bsz) * tc + r_idx // bsz).astype(BF16)
    to_batch_major = (c_idx == (r_idx % tc) * bsz + r_idx // tc).astype(BF16)
    ub = jnp.dot(to_time_major, u.astype(BF16), preferred_element_type=F32).astype(BF16)
    hw, hn = SSM_WIDTH // 2, SSM_LANES // 2
    for half in range(2):
        chans, lns = slice(half * hw, (half + 1) * hw), slice(half * hn, (half + 1) * hn)
        hre_ref[:, lns] = jnp.dot(ub[:, chans], wbre_ref[chans, lns], preferred_element_type=F32)
        him_ref[:, lns] = jnp.dot(ub[:, chans], wbim_ref[chans, lns], preferred_element_type=F32)

    width = SSM_SCAN_TILES * LANES
    for j0 in range(0, SSM_LANES, width):
        ls = slice(j0, j0 + width)
        are = jnp.broadcast_to(pre_ref[:, ls], (bsz, width))
        aim = jnp.broadcast_to(pim_ref[:, ls], (bsz, width))

        def tick(t, h):
            hr, hi = h
            r0 = pl.multiple_of(t * bsz, bsz)
            nr = (are * hr - aim * hi) + hre_ref[pl.ds(r0, bsz), ls]
            ni = (are * hi + aim * hr) + him_ref[pl.ds(r0, bsz), ls]
            hre_ref[pl.ds(r0, bsz), ls] = nr
            him_ref[pl.ds(r0, bsz), ls] = ni
            return nr, ni

        hr, hi = lax.fori_loop(0, tc, tick, (sre_ref[:, ls], sim_ref[:, ls]), unroll=SSM_SCAN_UNROLL)
        sre_ref[:, ls] = hr
        sim_ref[:, ls] = hi

    ys = []
    for half in range(2):
        cols, lns = slice(half * hw, (half + 1) * hw), slice(half * hn, (half + 1) * hn)
        ys.append(jnp.dot(hre_ref[:, lns].astype(BF16), wcre_ref[lns, cols], preferred_element_type=F32)
                  - jnp.dot(him_ref[:, lns].astype(BF16), wcim_ref[lns, cols], preferred_element_type=F32))
    y = _permute_rows_f32(to_batch_major, jnp.concatenate(ys, axis=1))
    y = y + d_ref[...] * u
    y = jax.nn.gelu(y)
    y = y * jax.nn.sigmoid(_bdot(y, wg_ref[...]) + bg_ref[...])
    o_ref[...] = (y * gb_ref[...].reshape(rows, sw)).reshape(bsz, tc, sw)


def _ssm_branch(u, gb, wbre, wbim, pre, pim, wcre, wcim, d_skip, w_glu, b_glu):
    bsz, seq, sw = u.shape
    assert bsz == SUBLANES, "the scan keeps the batch on the sublanes of one vreg tile"
    tc = SSM_CHUNK
    n = SSM_LANES
    row = pl.BlockSpec((bsz, tc, sw), lambda i: (0, i, 0))
    full = lambda r, c: pl.BlockSpec((r, c), lambda i: (0, 0))
    return pl.pallas_call(
        _ssm_kernel,
        grid=(seq // tc,),
        in_specs=[row, row, full(sw, n), full(sw, n), full(1, n), full(1, n),
                  full(n, sw), full(n, sw), full(1, sw), full(sw, sw), full(1, sw)],
        out_specs=row,
        out_shape=jax.ShapeDtypeStruct((bsz, seq, sw), F32),
        scratch_shapes=[pltpu.VMEM((bsz * tc, n), F32), pltpu.VMEM((bsz * tc, n), F32),
                        pltpu.VMEM((bsz, n), F32), pltpu.VMEM((bsz, n), F32)],
        compiler_params=pltpu.CompilerParams(
            dimension_semantics=("arbitrary",), vmem_limit_bytes=VMEM_LIMIT_BYTES),
    )(u, gb, wbre, wbim, pre, pim, wcre, wcim, d_skip.reshape(1, sw), w_glu.astype(BF16),
      b_glu.reshape(1, sw))


def _mid_kernel(x_ref, ya_ref, yb_ref, mod0_ref, mod1_ref, g_ref, wo_ref, wi_ref,
                x1_ref, q_ref, k_ref, v_ref, sg_ref):
    cw, aw = CONV_WIDTH, ATTN_WIDTH
    tl = x_ref.shape[1]
    mod1 = mod1_ref[0]
    scale = ATTN_HEAD_DIM ** -0.5 * LOG2_E
    halves = [slice(r, r + tl // 2) for r in (0, tl // 2)]
    ys = [jnp.dot(ya_ref[0, rows, :].astype(BF16), wo_ref[0:cw, :], preferred_element_type=F32)
          + jnp.dot(yb_ref[0, rows, :].astype(BF16), wo_ref[cw:, :], preferred_element_type=F32)
          for rows in halves]
    hs = []
    for rows, y in zip(halves, ys):
        x1 = x_ref[0, rows, :] + mod0_ref[0][2:3] * y
        x1_ref[0, rows, :] = x1
        hs.append(_rms_modulate(x1, g_ref[...], mod1[1:2], mod1[0:1]).astype(BF16))
    for rows, h in zip(halves, hs):
        q_ref[0, rows, :] = (jnp.dot(h, wi_ref[:, 0:aw], preferred_element_type=F32) * scale).astype(BF16)
        k_ref[0, rows, :] = jnp.dot(h, wi_ref[:, aw:2 * aw], preferred_element_type=F32).astype(BF16)
        v_ref[0, rows, :] = jnp.dot(h, wi_ref[:, 2 * aw:3 * aw], preferred_element_type=F32).astype(BF16)
        sg_ref[0, rows, :] = _silu(jnp.dot(h, wi_ref[:, 3 * aw:4 * aw], preferred_element_type=F32))


def _mid(x, ya, yb, mod0, mod1, norm_g, w_out, w_in):
    bsz, seq, d = x.shape
    tl = ROW_TILE
    row = lambda width: pl.BlockSpec((1, tl, width), lambda b, i: (b, i, 0))
    modspec = pl.BlockSpec((1, 3, d), lambda b, i: (b, 0, 0))
    full = lambda r, c: pl.BlockSpec((r, c), lambda b, i: (0, 0))
    aw = ATTN_WIDTH
    return pl.pallas_call(
        _mid_kernel,
        grid=(bsz, seq // tl),
        in_specs=[row(d), row(CONV_WIDTH), row(SSM_WIDTH), modspec, modspec, full(1, d),
                  full(CONV_WIDTH + SSM_WIDTH, d), full(d, 4 * aw)],
        out_specs=[row(d), row(aw), row(aw), row(aw), row(aw)],
        out_shape=[jax.ShapeDtypeStruct((bsz, seq, d), F32),
                   jax.ShapeDtypeStruct((bsz, seq, aw), BF16),
                   jax.ShapeDtypeStruct((bsz, seq, aw), BF16),
                   jax.ShapeDtypeStruct((bsz, seq, aw), BF16),
                   jax.ShapeDtypeStruct((bsz, seq, aw), F32)],
        compiler_params=pltpu.CompilerParams(
            dimension_semantics=("parallel", "parallel"), vmem_limit_bytes=VMEM_LIMIT_BYTES),
    )(x, ya, yb, mod0, mod1, norm_g.reshape(1, d), w_out.astype(BF16), w_in.astype(BF16))


def _attn_schedule(nq):
    visits = []
    for q in range(nq):
        visits.append((q, q, False))
        visits.append((q, q - 1, False) if q >= 1 else (0, 0, True))
    for q in range(2, nq):
        visits.extend((q, j, False) for j in range(q - 2, -1, -1))
    visits.append((0, 0, True))
    if len(visits) % 2:
        visits.append((0, 0, True))
    rows = []
    prev = (0, 0, True)
    for q, j, dummy in visits:
        pq, pj, pdummy = prev
        rows.append((q, j, nq if pdummy else pq, pj, int(pdummy)))
        prev = (q, j, dummy)
    return np.asarray(rows, np.int32).T.copy(), 2 * nq, len(visits)


def _attn_kernel(tbl_ref, q_ref, k_ref, v_ref, o_ref, acc_ref, carry_ref, qn_ref, qp_ref,
                 sp0_ref, sp1_ref, e0_ref, e1_ref, f0_ref, f1_ref, *, diag_steps, total_steps):
    blk = ATTN_BLOCK
    pair = 2 * ATTN_HEAD_DIM
    heads = range(2 * ATTN_PAIRS)
    cols = [slice((s // 2) * pair, (s // 2 + 1) * pair) for s in heads]
    nq = q_ref.shape[1] // blk
    lane = lax.broadcasted_iota(jnp.int32, (blk, pair), 1)
    first_head = lane < ATTN_HEAD_DIM
    r_idx = lax.broadcasted_iota(jnp.int32, (blk, blk), 0)
    c_idx = lax.broadcasted_iota(jnp.int32, (blk, blk), 1)
    after = (r_idx > c_idx).astype(BF16)
    causal = c_idx < r_idx
    nt = (((1,), (1,)), ((), ()))

    for qb in range(nq):
        for s in heads:
            q = q_ref[0, qb * blk:(qb + 1) * blk, cols[s]].astype(F32)
            own = first_head == (s % 2 == 0)
            qn_ref[qb, s] = jnp.where(own, -q, 0.0).astype(BF16)
            qp_ref[qb, s] = jnp.where(own, q, 0.0).astype(BF16)
    acc_ref[...] = jnp.zeros_like(acc_ref)
    carry_ref[...] = jnp.zeros_like(carry_ref)
    sp1_ref[...] = jnp.zeros_like(sp1_ref)
    e1_ref[...] = jnp.zeros_like(e1_ref)
    f1_ref[...] = jnp.zeros_like(f1_ref)

    def step(i, diagonal, sp_a, e_a, first_a, sp_b, e_b, first_b):
        a_q, a_key = tbl_ref[0, i], tbl_ref[1, i]
        b_slot, b_key, b_dummy = tbl_ref[2, i], tbl_ref[3, i], tbl_ref[4, i]
        a_start = pl.multiple_of(a_key * blk, blk)
        b_start = pl.multiple_of(b_key * blk, blk)
        penalty = jnp.where(b_dummy == 1, -MASKED_LOGIT, 0.0)
        sums = [jnp.dot(sp_b[s], after, preferred_element_type=F32) for s in heads]
        zs = [(lax.dot_general(qn_ref[a_q, s], k_ref[0, pl.ds(a_start, blk), cols[s]], nt,
                               preferred_element_type=F32),
               lax.dot_general(qp_ref[a_q, s], k_ref[0, pl.ds(a_start, blk), cols[s]], nt,
                               preferred_element_type=F32)) for s in heads]
        for s in heads:
            zn, zp = zs[s]
            l = jnp.minimum(zn, 0.0) - jnp.log(1.0 + jnp.exp2(jnp.minimum(zn, zp))) * LOG2_E
            neg_log_beta = zn - l
            if diagonal:
                l = jnp.where(causal, l, 0.0)
                neg_log_beta = jnp.where(causal, neg_log_beta, MASKED_LOGIT)
            sp_a[s] = l.astype(BF16)
            e_a[s] = neg_log_beta
            first_a[s] = l[:, 0:1]
        for s in heads:
            w = jnp.exp2((sums[s] + (carry_ref[b_slot, s] + penalty)) - e_b[s])
            acc_ref[b_slot, s] += jnp.dot(w.astype(BF16), v_ref[0, pl.ds(b_start, blk), cols[s]],
                                          preferred_element_type=F32)
            carry_ref[b_slot, s] += sums[s][:, 0:1] + first_b[s]

    def diag_pair(p, _):
        step(2 * p, True, sp0_ref, e0_ref, f0_ref, sp1_ref, e1_ref, f1_ref)
        step(2 * p + 1, False, sp1_ref, e1_ref, f1_ref, sp0_ref, e0_ref, f0_ref)
        return 0

    def plain_pair(p, _):
        step(2 * p, False, sp0_ref, e0_ref, f0_ref, sp1_ref, e1_ref, f1_ref)
        step(2 * p + 1, False, sp1_ref, e1_ref, f1_ref, sp0_ref, e0_ref, f0_ref)
        return 0

    lax.fori_loop(0, diag_steps // 2, diag_pair, 0)
    lax.fori_loop(diag_steps // 2, total_steps // 2, plain_pair, 0)
    for qb in range(nq):
        for p in range(ATTN_PAIRS):
            o_ref[0, qb * blk:(qb + 1) * blk, cols[2 * p]] = jnp.where(
                first_head, acc_ref[qb, 2 * p], acc_ref[qb, 2 * p + 1])


def _attention(q, k, v):
    bsz, seq, aw = q.shape
    pair = 2 * ATTN_HEAD_DIM
    nh = 2 * ATTN_PAIRS
    width = ATTN_PAIRS * pair
    blk = ATTN_BLOCK
    nq = seq // blk
    table, diag_steps, total_steps = _attn_schedule(nq)
    assert total_steps % 2 == 0 and table.shape[1] == total_steps
    seq_block = pl.BlockSpec((1, seq, width), lambda b, h, tbl: (b, 0, h))
    return pl.pallas_call(
        functools.partial(_attn_kernel, diag_steps=diag_steps, total_steps=total_steps),
        grid_spec=pltpu.PrefetchScalarGridSpec(
            num_scalar_prefetch=1,
            grid=(bsz, aw // width),
            in_specs=[seq_block, seq_block, seq_block],
            out_specs=seq_block,
            scratch_shapes=[pltpu.VMEM((nq + 1, nh, blk, pair), F32),
                            pltpu.VMEM((nq + 1, nh, blk, 1), F32),
                            pltpu.VMEM((nq, nh, blk, pair), BF16),
                            pltpu.VMEM((nq, nh, blk, pair), BF16),
                            pltpu.VMEM((nh, blk, blk), BF16), pltpu.VMEM((nh, blk, blk), BF16),
                            pltpu.VMEM((nh, blk, blk), F32), pltpu.VMEM((nh, blk, blk), F32),
                            pltpu.VMEM((nh, blk, 1), F32), pltpu.VMEM((nh, blk, 1), F32)]),
        out_shape=jax.ShapeDtypeStruct((bsz, seq, aw), F32),
        compiler_params=pltpu.CompilerParams(
            dimension_semantics=("parallel", "parallel"), vmem_limit_bytes=VMEM_LIMIT_BYTES),
    )(jnp.asarray(table), q, k, v)


def _final_kernel(x_ref, o_ref, sg_ref, mod_ref, wo_ref, g_ref, out_ref):
    y = _bdot(o_ref[0] * sg_ref[0], wo_ref[...])
    x2 = x_ref[0] + mod_ref[0][2:3] * y
    ms = jnp.mean(x2 * x2, axis=-1, keepdims=True)
    out_ref[0] = x2 * lax.rsqrt(ms + EPS) * g_ref[...]


def _final(x1, o, sg, mod1, w_out, final_g):
    bsz, seq, d = x1.shape
    tl = ROW_TILE
    row = pl.BlockSpec((1, tl, d), lambda b, i: (b, i, 0))
    return pl.pallas_call(
        _final_kernel,
        grid=(bsz, seq // tl),
        in_specs=[row, row, row,
                  pl.BlockSpec((1, 3, d), lambda b, i: (b, 0, 0)),
                  pl.BlockSpec((ATTN_WIDTH, d), lambda b, i: (0, 0)),
                  pl.BlockSpec((1, d), lambda b, i: (0, 0))],
        out_specs=row,
        out_shape=jax.ShapeDtypeStruct((bsz, seq, d), F32),
        compiler_params=pltpu.CompilerParams(dimension_semantics=("parallel", "parallel")),
    )(x1, o, sg, mod1, w_out.astype(BF16), final_g.reshape(1, d))


def kernel(x, c, l0_norm_g, l0_w_ada, l0_b_ada, l0_w_in, l0_conv_w, l0_conv_b, l0_conv_ln_g, l0_conv_ln_b, l0_ssm_lam_re, l0_ssm_lam_im, l0_ssm_log_dt, l0_ssm_b_re, l0_ssm_b_im, l0_ssm_c_re, l0_ssm_c_im, l0_ssm_d, l0_ssm_w_glu, l0_ssm_b_glu, l0_w_out, l1_norm_g, l1_w_ada, l1_b_ada, l1_w_in, l1_w_out, final_norm_g):
    bsz, seq, d = x.shape
    mod0 = _ada(c, l0_w_ada, l0_b_ada).reshape(bsz, 3, d)
    mod1 = _ada(c, l1_w_ada, l1_b_ada).reshape(bsz, 3, d)

    hg, ga, u, gb = _even_in(x, mod0, l0_norm_g, l0_w_in)
    ya = _conv_branch(hg, ga, l0_conv_w, l0_conv_b, l0_conv_ln_g, l0_conv_ln_b)

    pre, pim, bbre, bbim = _ssm_prep(l0_ssm_lam_re, l0_ssm_lam_im, l0_ssm_log_dt, l0_ssm_b_re, l0_ssm_b_im)
    yb = _ssm_branch(u, gb, _block_diag_in(bbre), _block_diag_in(bbim), pre, pim,
                     _block_diag_out(l0_ssm_c_re), _block_diag_out(l0_ssm_c_im),
                     l0_ssm_d, l0_ssm_w_glu, l0_ssm_b_glu)

    x1, q, k, v, sg = _mid(x, ya, yb, mod0, mod1, l1_norm_g, l0_w_out, l1_w_in)
    o = _attention(q, k, v)
    return _final(x1, o, sg, mod1, l1_w_out, final_norm_g)
```

```python
import functools
import math

import jax
import jax.numpy as jnp
from jax import lax
import numpy as np
from jax.experimental import pallas as pl
from jax.experimental.pallas import tpu as pltpu

F32 = jnp.float32
BF16 = jnp.bfloat16

D_MODEL = 1024
CONV_WIDTH = 1024
CONV_KERNEL = 31
SSM_WIDTH = 512
SSM_GROUP = 16
SSM_GROUPS = SSM_WIDTH // SSM_GROUP
SSM_STATE = 64
SSM_LANES = SSM_GROUPS * SSM_STATE
ATTN_HEADS = 16
ATTN_HEAD_DIM = 64
ATTN_WIDTH = ATTN_HEADS * ATTN_HEAD_DIM
EPS = 1e-6
LOG2_E = math.log2(math.e)
MASKED_LOGIT = 1e30

SUBLANES = 8
LANES = 128
VMEM_LIMIT_BYTES = 56 * 1024 * 1024

ROW_TILE = 512
CONV_TILE = 256
CONV_HALO = 32
CONV_CHUNK = 32
SSM_CHUNK = 32
SSM_SCAN_TILES = 8
SSM_SCAN_UNROLL = 4
ATTN_BLOCK = 256
ATTN_PAIRS = 2


def _silu(x):
    return x * jax.nn.sigmoid(x)


def _bdot(a, b):
    return jnp.dot(a.astype(BF16), b.astype(BF16), preferred_element_type=F32)


def _rms_modulate(x, g, scale, shift):
    ms = jnp.mean(x * x, axis=-1, keepdims=True)
    return (x * lax.rsqrt(ms + EPS) * g) * (1.0 + scale) + shift


def _ada_kernel(c_ref, w_ref, b_ref, o_ref):
    o_ref[...] = _bdot(_silu(c_ref[...]), w_ref[...]) + b_ref[...]


def _ada(c, w_ada, b_ada):
    bsz, d = c.shape
    n = w_ada.shape[1]
    tn = d
    return pl.pallas_call(
        _ada_kernel,
        grid=(n // tn,),
        in_specs=[pl.BlockSpec((bsz, d), lambda j: (0, 0)),
                  pl.BlockSpec((d, tn), lambda j: (0, j)),
                  pl.BlockSpec((1, tn), lambda j: (0, j))],
        out_specs=pl.BlockSpec((bsz, tn), lambda j: (0, j)),
        out_shape=jax.ShapeDtypeStruct((bsz, n), F32),
        compiler_params=pltpu.CompilerParams(dimension_semantics=("parallel",)),
    )(c, w_ada, b_ada.reshape(1, n))


def _even_in_kernel(x_ref, mod_ref, g_ref, w_ref, hg_ref, ga_ref, u_ref, gb_ref):
    mod = mod_ref[0]
    cw, sw = CONV_WIDTH, SSM_WIDTH
    tl = x_ref.shape[1]
    halves = [slice(r, r + tl // 2) for r in (0, tl // 2)]
    hs = [_rms_modulate(x_ref[0, rows, :], g_ref[...], mod[1:2], mod[0:1]).astype(BF16) for rows in halves]
    for rows, h in zip(halves, hs):
        val = jnp.dot(h, w_ref[:, 0:cw], preferred_element_type=F32)
        glu = jnp.dot(h, w_ref[:, cw:2 * cw], preferred_element_type=F32)
        hg_ref[0, rows, :] = val * jax.nn.sigmoid(glu)
        ga_ref[0, rows, :] = _silu(jnp.dot(h, w_ref[:, 2 * cw:3 * cw], preferred_element_type=F32))
        u_ref[0, rows, :] = jnp.dot(h, w_ref[:, 3 * cw:3 * cw + sw], preferred_element_type=F32)
        gb_ref[0, rows, :] = _silu(jnp.dot(h, w_ref[:, 3 * cw + sw:3 * cw + 2 * sw], preferred_element_type=F32))


def _even_in(x, mod, norm_g, w_in):
    bsz, seq, d = x.shape
    tl = ROW_TILE
    n_in = w_in.shape[1]
    row = lambda width: pl.BlockSpec((1, tl, width), lambda b, i: (b, i, 0))
    return pl.pallas_call(
        _even_in_kernel,
        grid=(bsz, seq // tl),
        in_specs=[row(d),
                  pl.BlockSpec((1, 3, d), lambda b, i: (b, 0, 0)),
                  pl.BlockSpec((1, d), lambda b, i: (0, 0)),
                  pl.BlockSpec((d, n_in), lambda b, i: (0, 0))],
        out_specs=[row(CONV_WIDTH), row(CONV_WIDTH), row(SSM_WIDTH), row(SSM_WIDTH)],
        out_shape=[jax.ShapeDtypeStruct((bsz, seq, CONV_WIDTH), F32),
                   jax.ShapeDtypeStruct((bsz, seq, CONV_WIDTH), F32),
                   jax.ShapeDtypeStruct((bsz, seq, SSM_WIDTH), F32),
                   jax.ShapeDtypeStruct((bsz, seq, SSM_WIDTH), F32)],
        compiler_params=pltpu.CompilerParams(
            dimension_semantics=("parallel", "parallel"), vmem_limit_bytes=VMEM_LIMIT_BYTES),
    )(x, mod, norm_g.reshape(1, d), w_in.astype(BF16))


def _conv_kernel(cur_ref, halo_ref, ga_ref, w_ref, cb_ref, lg_ref, lb_ref, o_ref, buf_ref, sh_ref):
    tl = cur_ref.shape[1]
    first = pl.program_id(1) == 0
    buf_ref[0:CONV_HALO, :] = jnp.where(first, 0.0, halo_ref[0])
    buf_ref[CONV_HALO:CONV_HALO + tl, :] = cur_ref[0]
    lead = CONV_HALO - (CONV_KERNEL - 1)
    rows = sh_ref.shape[1]
    for r in range(1, SUBLANES):
        sh_ref[r - 1] = buf_ref[r:r + rows, :]

    groups = CONV_CHUNK // SUBLANES
    for r0 in range(0, tl, CONV_CHUNK):
        acc = jnp.zeros((groups, SUBLANES, CONV_WIDTH), F32)
        for k in range(CONV_KERNEL):
            r, base = (lead + k) % SUBLANES, r0 + (lead + k) // SUBLANES * SUBLANES
            src = buf_ref if r == 0 else sh_ref.at[r - 1]
            rows8 = src[base:base + CONV_CHUNK, :].reshape(groups, SUBLANES, CONV_WIDTH)
            acc = acc + rows8 * w_ref[k * SUBLANES:(k + 1) * SUBLANES, :][None]
        acc = acc.reshape(CONV_CHUNK, CONV_WIDTH) + cb_ref[...]
        mu = jnp.mean(acc, axis=-1, keepdims=True)
        cen = acc - mu
        var = jnp.mean(cen * cen, axis=-1, keepdims=True)
        y = cen * lax.rsqrt(var + EPS) * lg_ref[...] + lb_ref[...]
        o_ref[0, r0:r0 + CONV_CHUNK, :] = (_silu(y) * ga_ref[0, r0:r0 + CONV_CHUNK, :]).astype(BF16)


def _conv_branch(hg, ga, conv_w, conv_b, ln_g, ln_b):
    bsz, seq, cw = hg.shape
    tl = CONV_TILE
    halo_per_tile = tl // CONV_HALO
    row = pl.BlockSpec((1, tl, cw), lambda b, i: (b, i, 0))
    vec = pl.BlockSpec((1, cw), lambda b, i: (0, 0))
    return pl.pallas_call(
        _conv_kernel,
        grid=(bsz, seq // tl),
        in_specs=[row,
                  pl.BlockSpec((1, CONV_HALO, cw),
                               lambda b, i: (b, jnp.maximum(i * halo_per_tile - 1, 0), 0)),
                  row,
                  pl.BlockSpec((CONV_KERNEL * SUBLANES, cw), lambda b, i: (0, 0)),
                  vec, vec, vec],
        out_specs=row,
        out_shape=jax.ShapeDtypeStruct((bsz, seq, cw), BF16),
        scratch_shapes=[pltpu.VMEM((CONV_HALO + tl, cw), F32),
                        pltpu.VMEM((SUBLANES - 1, CONV_HALO + tl - SUBLANES, cw), F32)],
        compiler_params=pltpu.CompilerParams(dimension_semantics=("parallel", "parallel")),
    )(hg, hg, ga, jnp.repeat(conv_w, SUBLANES, axis=0), conv_b.reshape(1, cw), ln_g.reshape(1, cw),
      ln_b.reshape(1, cw))


def _ssm_prep_kernel(lre_ref, lim_ref, ldt_ref, bre_ref, bim_ref, pre_ref, pim_ref, bbre_ref, bbim_ref):
    lre, lim = lre_ref[...], lim_ref[...]
    dt = jnp.exp(ldt_ref[...])
    mag = jnp.exp(lre * dt)
    are, aim = mag * jnp.cos(lim * dt), mag * jnp.sin(lim * dt)
    nre, nim = are - 1.0, aim
    den = lre * lre + lim * lim
    cre = (nre * lre + nim * lim) / den
    cim = (nim * lre - nre * lim) / den
    bre, bim = bre_ref[...], bim_ref[...]
    bbre_ref[...] = cre * bre - cim * bim
    bbim_ref[...] = cre * bim + cim * bre
    pre_ref[...] = are
    pim_ref[...] = aim


def _ssm_prep(lam_re, lam_im, log_dt, b_re, b_im):
    n = SSM_LANES
    flat = lambda t: t.reshape(1, n)
    to_lanes = lambda t: t.reshape(n, SSM_GROUP).T
    ldt = jnp.broadcast_to(log_dt[:, None], (SSM_GROUPS, SSM_STATE))
    outs = pl.pallas_call(
        _ssm_prep_kernel,
        out_shape=[jax.ShapeDtypeStruct((1, n), F32), jax.ShapeDtypeStruct((1, n), F32),
                   jax.ShapeDtypeStruct((SSM_GROUP, n), F32), jax.ShapeDtypeStruct((SSM_GROUP, n), F32)],
    )(flat(lam_re), flat(lam_im), flat(ldt), to_lanes(b_re), to_lanes(b_im))
    return outs


def _group_mask_in():
    r = jnp.arange(SSM_WIDTH)[:, None] // SSM_GROUP
    c = jnp.arange(SSM_LANES)[None, :] // SSM_STATE
    return r == c


def _block_diag_in(bb):
    return jnp.where(_group_mask_in(), jnp.tile(bb, (SSM_GROUPS, 1)), 0.0).astype(BF16)


def _block_diag_out(cmat):
    per_lane = cmat.transpose(0, 2, 1).reshape(SSM_LANES, SSM_GROUP)
    return jnp.where(_group_mask_in().T, jnp.tile(per_lane, (1, SSM_GROUPS)), 0.0).astype(BF16)


def _permute_rows_f32(perm, x):
    hi = x.astype(BF16)
    rest = x - hi.astype(F32)
    mid = rest.astype(BF16)
    lo = (rest - mid.astype(F32)).astype(BF16)
    return (jnp.dot(perm, hi, preferred_element_type=F32) + jnp.dot(perm, mid, preferred_element_type=F32)
            + jnp.dot(perm, lo, preferred_element_type=F32))


def _ssm_kernel(u_ref, gb_ref, wbre_ref, wbim_ref, pre_ref, pim_ref, wcre_ref, wcim_ref,
                d_ref, wg_ref, bg_ref, o_ref, hre_ref, him_ref, sre_ref, sim_ref):
    bsz, tc, sw = u_ref.shape
    rows = bsz * tc

    @pl.when(pl.program_id(0) == 0)
    def _():
        sre_ref[...] = jnp.zeros_like(sre_ref)
        sim_ref[...] = jnp.zeros_like(sim_ref)

    u = u_ref[...].reshape(rows, sw)
    r_idx = lax.broadcasted_iota(jnp.int32, (rows, rows), 0)
    c_idx = lax.broadcasted_iota(jnp.int32, (rows, rows), 1)
    to_time_major = (c_idx == (r_idx % bsz) * tc + r_idx // bsz).astype(BF16)
    to_batch_major = (c_idx == (r_idx % tc) * bsz + r_idx // tc).astype(BF16)
    ub = jnp.dot(to_time_major, u.astype(BF16), preferred_element_type=F32).astype(BF16)
    hw, hn = SSM_WIDTH // 2, SSM_LANES // 2
    for half in range(2):
        chans, lns = slice(half * hw, (half + 1) * hw), slice(half * hn, (half + 1) * hn)
        hre_ref[:, lns] = jnp.dot(ub[:, chans], wbre_ref[chans, lns], preferred_element_type=F32)
        him_ref[:, lns] = jnp.dot(ub[:, chans], wbim_ref[chans, lns], preferred_element_type=F32)

    width = SSM_SCAN_TILES * LANES
    for j0 in range(0, SSM_LANES, width):
        ls = slice(j0, j0 + width)
        are = jnp.broadcast_to(pre_ref[:, ls], (bsz, width))
        aim = jnp.broadcast_to(pim_ref[:, ls], (bsz, width))

        def tick(t, h):
            hr, hi = h
            r0 = pl.multiple_of(t * bsz, bsz)
            nr = (are * hr - aim * hi) + hre_ref[pl.ds(r0, bsz), ls]
            ni = (are * hi + aim * hr) + him_ref[pl.ds(r0, bsz), ls]
            hre_ref[pl.ds(r0, bsz), ls] = nr
            him_ref[pl.ds(r0, bsz), ls] = ni
            return nr, ni

        hr, hi = lax.fori_loop(0, tc, tick, (sre_ref[:, ls], sim_ref[:, ls]), unroll=SSM_SCAN_UNROLL)
        sre_ref[:, ls] = hr
        sim_ref[:, ls] = hi

    ys = []
    for half in range(2):
        cols, lns = slice(half * hw, (half + 1) * hw), slice(half * hn, (half + 1) * hn)
        ys.append(jnp.dot(hre_ref[:, lns].astype(BF16), wcre_ref[lns, cols], preferred_element_type=F32)
                  - jnp.dot(him_ref[:, lns].astype(BF16), wcim_ref[lns, cols], preferred_element_type=F32))
    y = _permute_rows_f32(to_batch_major, jnp.concatenate(ys, axis=1))
    y = y + d_ref[...] * u
    y = jax.nn.gelu(y)
    y = y * jax.nn.sigmoid(_bdot(y, wg_ref[...]) + bg_ref[...])
    o_ref[...] = (y * gb_ref[...].reshape(rows, sw)).reshape(bsz, tc, sw).astype(BF16)


def _ssm_branch(u, gb, wbre, wbim, pre, pim, wcre, wcim, d_skip, w_glu, b_glu):
    bsz, seq, sw = u.shape
    assert bsz == SUBLANES, "the scan keeps the batch on the sublanes of one vreg tile"
    tc = SSM_CHUNK
    n = SSM_LANES
    row = pl.BlockSpec((bsz, tc, sw), lambda i: (0, i, 0))
    full = lambda r, c: pl.BlockSpec((r, c), lambda i: (0, 0))
    return pl.pallas_call(
        _ssm_kernel,
        grid=(seq // tc,),
        in_specs=[row, row, full(sw, n), full(sw, n), full(1, n), full(1, n),
                  full(n, sw), full(n, sw), full(1, sw), full(sw, sw), full(1, sw)],
        out_specs=row,
        out_shape=jax.ShapeDtypeStruct((bsz, seq, sw), BF16),
        scratch_shapes=[pltpu.VMEM((bsz * tc, n), F32), pltpu.VMEM((bsz * tc, n), F32),
                        pltpu.VMEM((bsz, n), F32), pltpu.VMEM((bsz, n), F32)],
        compiler_params=pltpu.CompilerParams(
            dimension_semantics=("arbitrary",), vmem_limit_bytes=VMEM_LIMIT_BYTES),
    )(u, gb, wbre, wbim, pre, pim, wcre, wcim, d_skip.reshape(1, sw), w_glu.astype(BF16),
      b_glu.reshape(1, sw))


def _mid_kernel(x_ref, ya_ref, yb_ref, mod0_ref, mod1_ref, g_ref, wo_ref, wi_ref,
                x1_ref, q_ref, k_ref, v_ref, sg_ref):
    cw, aw = CONV_WIDTH, ATTN_WIDTH
    tl = x_ref.shape[1]
    mod1 = mod1_ref[0]
    scale = ATTN_HEAD_DIM ** -0.5 * LOG2_E
    halves = [slice(r, r + tl // 2) for r in (0, tl // 2)]
    ys = [jnp.dot(ya_ref[0, rows, :].astype(BF16), wo_ref[0:cw, :], preferred_element_type=F32)
          + jnp.dot(yb_ref[0, rows, :].astype(BF16), wo_ref[cw:, :], preferred_element_type=F32)
          for rows in halves]
    hs = []
    for rows, y in zip(halves, ys):
        x1 = x_ref[0, rows, :] + mod0_ref[0][2:3] * y
        x1_ref[0, rows, :] = x1
        hs.append(_rms_modulate(x1, g_ref[...], mod1[1:2], mod1[0:1]).astype(BF16))
    for rows, h in zip(halves, hs):
        q_ref[0, rows, :] = (jnp.dot(h, wi_ref[:, 0:aw], preferred_element_type=F32) * scale).astype(BF16)
        k_ref[0, rows, :] = jnp.dot(h, wi_ref[:, aw:2 * aw], preferred_element_type=F32).astype(BF16)
        v_ref[0, rows, :] = jnp.dot(h, wi_ref[:, 2 * aw:3 * aw], preferred_element_type=F32).astype(BF16)
        sg_ref[0, rows, :] = _silu(jnp.dot(h, wi_ref[:, 3 * aw:4 * aw], preferred_element_type=F32)).astype(BF16)


def _mid(x, ya, yb, mod0, mod1, norm_g, w_out, w_in):
    bsz, seq, d = x.shape
    tl = ROW_TILE
    row = lambda width: pl.BlockSpec((1, tl, width), lambda b, i: (b, i, 0))
    modspec = pl.BlockSpec((1, 3, d), lambda b, i: (b, 0, 0))
    full = lambda r, c: pl.BlockSpec((r, c), lambda b, i: (0, 0))
    aw = ATTN_WIDTH
    return pl.pallas_call(
        _mid_kernel,
        grid=(bsz, seq // tl),
        in_specs=[row(d), row(CONV_WIDTH), row(SSM_WIDTH), modspec, modspec, full(1, d),
                  full(CONV_WIDTH + SSM_WIDTH, d), full(d, 4 * aw)],
        out_specs=[row(d), row(aw), row(aw), row(aw), row(aw)],
        out_shape=[jax.ShapeDtypeStruct((bsz, seq, d), F32),
                   jax.ShapeDtypeStruct((bsz, seq, aw), BF16),
                   jax.ShapeDtypeStruct((bsz, seq, aw), BF16),
                   jax.ShapeDtypeStruct((bsz, seq, aw), BF16),
                   jax.ShapeDtypeStruct((bsz, seq, aw), BF16)],
        compiler_params=pltpu.CompilerParams(
            dimension_semantics=("parallel", "parallel"), vmem_limit_bytes=VMEM_LIMIT_BYTES),
    )(x, ya, yb, mod0, mod1, norm_g.reshape(1, d), w_out.astype(BF16), w_in.astype(BF16))


def _attn_schedule(nq):
    visits = []
    for q in range(nq):
        visits.append((q, q, False))
        visits.append((q, q - 1, False) if q >= 1 else (0, 0, True))
    for q in range(2, nq):
        visits.extend((q, j, False) for j in range(q - 2, -1, -1))
    visits.append((0, 0, True))
    if len(visits) % 2:
        visits.append((0, 0, True))
    rows = []
    prev = (0, 0, True)
    for q, j, dummy in visits:
        pq, pj, pdummy = prev
        rows.append((q, j, nq if pdummy else pq, pj, int(pdummy)))
        prev = (q, j, dummy)
    return np.asarray(rows, np.int32).T.copy(), 2 * nq, len(visits)


def _attn_kernel(tbl_ref, q_ref, k_ref, v_ref, o_ref, acc_ref, carry_ref, qn_ref, qp_ref,
                 sp0_ref, sp1_ref, e0_ref, e1_ref, f0_ref, f1_ref, *, diag_steps, total_steps):
    blk = ATTN_BLOCK
    pair = 2 * ATTN_HEAD_DIM
    heads = range(2 * ATTN_PAIRS)
    cols = [slice((s // 2) * pair, (s // 2 + 1) * pair) for s in heads]
    nq = q_ref.shape[1] // blk
    lane = lax.broadcasted_iota(jnp.int32, (blk, pair), 1)
    first_head = lane < ATTN_HEAD_DIM
    r_idx = lax.broadcasted_iota(jnp.int32, (blk, blk), 0)
    c_idx = lax.broadcasted_iota(jnp.int32, (blk, blk), 1)
    after = (r_idx > c_idx).astype(BF16)
    causal = c_idx < r_idx
    nt = (((1,), (1,)), ((), ()))

    for qb in range(nq):
        for s in heads:
            q = q_ref[0, qb * blk:(qb + 1) * blk, cols[s]].astype(F32)
            own = first_head == (s % 2 == 0)
            qn_ref[qb, s] = jnp.where(own, -q, 0.0).astype(BF16)
            qp_ref[qb, s] = jnp.where(own, q, 0.0).astype(BF16)
    acc_ref[...] = jnp.zeros_like(acc_ref)
    carry_ref[...] = jnp.zeros_like(carry_ref)
    sp1_ref[...] = jnp.zeros_like(sp1_ref)
    e1_ref[...] = jnp.zeros_like(e1_ref)
    f1_ref[...] = jnp.zeros_like(f1_ref)

    def step(i, diagonal, sp_a, e_a, first_a, sp_b, e_b, first_b):
        a_q, a_key = tbl_ref[0, i], tbl_ref[1, i]
        b_slot, b_key, b_dummy = tbl_ref[2, i], tbl_ref[3, i], tbl_ref[4, i]
        a_start = pl.multiple_of(a_key * blk, blk)
        b_start = pl.multiple_of(b_key * blk, blk)
        penalty = jnp.where(b_dummy == 1, -MASKED_LOGIT, 0.0)
        sums = [jnp.dot(sp_b[s], after, preferred_element_type=F32) for s in heads]
        zs = [(lax.dot_general(qn_ref[a_q, s], k_ref[0, pl.ds(a_start, blk), cols[s]], nt,
                               preferred_element_type=F32),
               lax.dot_general(qp_ref[a_q, s], k_ref[0, pl.ds(a_start, blk), cols[s]], nt,
                               preferred_element_type=F32)) for s in heads]
        for s in heads:
            zn, zp = zs[s]
            l = jnp.minimum(zn, 0.0) - jnp.log(1.0 + jnp.exp2(jnp.minimum(zn, zp))) * LOG2_E
            neg_log_beta = zn - l
            if diagonal:
                l = jnp.where(causal, l, 0.0)
                neg_log_beta = jnp.where(causal, neg_log_beta, MASKED_LOGIT)
            sp_a[s] = l.astype(BF16)
            e_a[s] = neg_log_beta
            first_a[s] = l[:, 0:1]
        for s in heads:
            w = jnp.exp2((sums[s] + (carry_ref[b_slot, s] + penalty)) - e_b[s])
            acc_ref[b_slot, s] += jnp.dot(w.astype(BF16), v_ref[0, pl.ds(b_start, blk), cols[s]],
                                          preferred_element_type=F32)
            carry_ref[b_slot, s] += sums[s][:, 0:1] + first_b[s]

    def diag_pair(p, _):
        step(2 * p, True, sp0_ref, e0_ref, f0_ref, sp1_ref, e1_ref, f1_ref)
        step(2 * p + 1, False, sp1_ref, e1_ref, f1_ref, sp0_ref, e0_ref, f0_ref)
        return 0

    def plain_pair(p, _):
        step(2 * p, False, sp0_ref, e0_ref, f0_ref, sp1_ref, e1_ref, f1_ref)
        step(2 * p + 1, False, sp1_ref, e1_ref, f1_ref, sp0_ref, e0_ref, f0_ref)
        return 0

    lax.fori_loop(0, diag_steps // 2, diag_pair, 0)
    lax.fori_loop(diag_steps // 2, total_steps // 2, plain_pair, 0)
    for qb in range(nq):
        for p in range(ATTN_PAIRS):
            o_ref[0, qb * blk:(qb + 1) * blk, cols[2 * p]] = jnp.where(
                first_head, acc_ref[qb, 2 * p], acc_ref[qb, 2 * p + 1]).astype(o_ref.dtype)


def _attention(q, k, v):
    bsz, seq, aw = q.shape
    pair = 2 * ATTN_HEAD_DIM
    nh = 2 * ATTN_PAIRS
    width = ATTN_PAIRS * pair
    blk = ATTN_BLOCK
    nq = seq // blk
    table, diag_steps, total_steps = _attn_schedule(nq)
    assert total_steps % 2 == 0 and table.shape[1] == total_steps
    seq_block = pl.BlockSpec((1, seq, width), lambda b, h, tbl: (b, 0, h))
    return pl.pallas_call(
        functools.partial(_attn_kernel, diag_steps=diag_steps, total_steps=total_steps),
        grid_spec=pltpu.PrefetchScalarGridSpec(
            num_scalar_prefetch=1,
            grid=(bsz, aw // width),
            in_specs=[seq_block, seq_block, seq_block],
            out_specs=seq_block,
            scratch_shapes=[pltpu.VMEM((nq + 1, nh, blk, pair), F32),
                            pltpu.VMEM((nq + 1, nh, blk, 1), F32),
                            pltpu.VMEM((nq, nh, blk, pair), BF16),
                            pltpu.VMEM((nq, nh, blk, pair), BF16),
                            pltpu.VMEM((nh, blk, blk), BF16), pltpu.VMEM((nh, blk, blk), BF16),
                            pltpu.VMEM((nh, blk, blk), F32), pltpu.VMEM((nh, blk, blk), F32),
                            pltpu.VMEM((nh, blk, 1), F32), pltpu.VMEM((nh, blk, 1), F32)]),
        out_shape=jax.ShapeDtypeStruct((bsz, seq, aw), BF16),
        compiler_params=pltpu.CompilerParams(
            dimension_semantics=("parallel", "parallel"), vmem_limit_bytes=VMEM_LIMIT_BYTES),
    )(jnp.asarray(table), q, k, v)


def _final_kernel(x_ref, o_ref, sg_ref, mod_ref, wo_ref, g_ref, out_ref):
    y = _bdot(o_ref[0].astype(F32) * sg_ref[0].astype(F32), wo_ref[...])
    x2 = x_ref[0] + mod_ref[0][2:3] * y
    ms = jnp.mean(x2 * x2, axis=-1, keepdims=True)
    out_ref[0] = x2 * lax.rsqrt(ms + EPS) * g_ref[...]


def _final(x1, o, sg, mod1, w_out, final_g):
    bsz, seq, d = x1.shape
    tl = ROW_TILE
    row = pl.BlockSpec((1, tl, d), lambda b, i: (b, i, 0))
    return pl.pallas_call(
        _final_kernel,
        grid=(bsz, seq // tl),
        in_specs=[row, row, row,
                  pl.BlockSpec((1, 3, d), lambda b, i: (b, 0, 0)),
                  pl.BlockSpec((ATTN_WIDTH, d), lambda b, i: (0, 0)),
                  pl.BlockSpec((1, d), lambda b, i: (0, 0))],
        out_specs=row,
        out_shape=jax.ShapeDtypeStruct((bsz, seq, d), F32),
        compiler_params=pltpu.CompilerParams(dimension_semantics=("parallel", "parallel")),
    )(x1, o, sg, mod1, w_out.astype(BF16), final_g.reshape(1, d))


def kernel(x, c, l0_norm_g, l0_w_ada, l0_b_ada, l0_w_in, l0_conv_w, l0_conv_b, l0_conv_ln_g, l0_conv_ln_b, l0_ssm_lam_re, l0_ssm_lam_im, l0_ssm_log_dt, l0_ssm_b_re, l0_ssm_b_im, l0_ssm_c_re, l0_ssm_c_im, l0_ssm_d, l0_ssm_w_glu, l0_ssm_b_glu, l0_w_out, l1_norm_g, l1_w_ada, l1_b_ada, l1_w_in, l1_w_out, final_norm_g):
    bsz, seq, d = x.shape
    mod0 = _ada(c, l0_w_ada, l0_b_ada).reshape(bsz, 3, d)
    mod1 = _ada(c, l1_w_ada, l1_b_ada).reshape(bsz, 3, d)

    hg, ga, u, gb = _even_in(x, mod0, l0_norm_g, l0_w_in)
    ya = _conv_branch(hg, ga, l0_conv_w, l0_conv_b, l0_conv_ln_g, l0_conv_ln_b)

    pre, pim, bbre, bbim = _ssm_prep(l0_ssm_lam_re, l0_ssm_lam_im, l0_ssm_log_dt, l0_ssm_b_re, l0_ssm_b_im)
    yb = _ssm_branch(u, gb, _block_diag_in(bbre), _block_diag_in(bbim), pre, pim,
                     _block_diag_out(l0_ssm_c_re), _block_diag_out(l0_ssm_c_im),
                     l0_ssm_d, l0_ssm_w_glu, l0_ssm_b_glu)

    x1, q, k, v, sg = _mid(x, ya, yb, mod0, mod1, l1_norm_g, l0_w_out, l1_w_in)
    o = _attention(q, k, v)
    return _final(x1, o, sg, mod1, l1_w_out, final_norm_g)
```

```python
import functools
import math

import jax
import jax.numpy as jnp
from jax import lax
import numpy as np
from jax.experimental import pallas as pl
from jax.experimental.pallas import tpu as pltpu

F32 = jnp.float32
BF16 = jnp.bfloat16

D_MODEL = 1024
CONV_WIDTH = 1024
CONV_KERNEL = 31
SSM_WIDTH = 512
SSM_GROUP = 16
SSM_GROUPS = SSM_WIDTH // SSM_GROUP
SSM_STATE = 64
SSM_LANES = SSM_GROUPS * SSM_STATE
ATTN_HEADS = 16
ATTN_HEAD_DIM = 64
ATTN_WIDTH = ATTN_HEADS * ATTN_HEAD_DIM
EPS = 1e-6
LOG2_E = math.log2(math.e)
MASKED_LOGIT = 1e30

SUBLANES = 8
LANES = 128
VMEM_LIMIT_BYTES = 56 * 1024 * 1024

ROW_TILE = 512
CONV_TILE = 256
CONV_HALO = 32
CONV_CHUNK = 32
SSM_CHUNK = 32
SSM_SCAN_TILES = 8
SSM_SCAN_UNROLL = 4
ATTN_BLOCK = 256
ATTN_PAIRS = 2


def _silu(x):
    return x * jax.nn.sigmoid(x)


def _bdot(a, b):
    return jnp.dot(a.astype(BF16), b.astype(BF16), preferred_element_type=F32)


def _rms_modulate(x, g, scale, shift):
    ms = jnp.mean(x * x, axis=-1, keepdims=True)
    return (x * lax.rsqrt(ms + EPS) * g) * (1.0 + scale) + shift


def _ada_kernel(c_ref, w_ref, b_ref, o_ref):
    o_ref[...] = _bdot(_silu(c_ref[...]), w_ref[...]) + b_ref[...]


def _ada(c, w_ada, b_ada):
    bsz, d = c.shape
    n = w_ada.shape[1]
    tn = d
    return pl.pallas_call(
        _ada_kernel,
        grid=(n // tn,),
        in_specs=[pl.BlockSpec((bsz, d), lambda j: (0, 0)),
                  pl.BlockSpec((d, tn), lambda j: (0, j)),
                  pl.BlockSpec((1, tn), lambda j: (0, j))],
        out_specs=pl.BlockSpec((bsz, tn), lambda j: (0, j)),
        out_shape=jax.ShapeDtypeStruct((bsz, n), F32),
        compiler_params=pltpu.CompilerParams(dimension_semantics=("parallel",)),
    )(c, w_ada, b_ada.reshape(1, n))


def _even_in_kernel(x_ref, mod_ref, g_ref, w_ref, hg_ref, ga_ref, u_ref, gb_ref):
    mod = mod_ref[0]
    cw, sw = CONV_WIDTH, SSM_WIDTH
    tl = x_ref.shape[1]
    halves = [slice(r, r + tl // 2) for r in (0, tl // 2)]
    hs = [_rms_modulate(x_ref[0, rows, :], g_ref[...], mod[1:2], mod[0:1]).astype(BF16) for rows in halves]
    for rows, h in zip(halves, hs):
        val = jnp.dot(h, w_ref[:, 0:cw], preferred_element_type=F32)
        glu = jnp.dot(h, w_ref[:, cw:2 * cw], preferred_element_type=F32)
        hg_ref[0, rows, :] = val * jax.nn.sigmoid(glu)
        ga_ref[0, rows, :] = _silu(jnp.dot(h, w_ref[:, 2 * cw:3 * cw], preferred_element_type=F32))
        u_ref[0, rows, :] = jnp.dot(h, w_ref[:, 3 * cw:3 * cw + sw], preferred_element_type=F32)
        gb_ref[0, rows, :] = _silu(jnp.dot(h, w_ref[:, 3 * cw + sw:3 * cw + 2 * sw], preferred_element_type=F32))


def _even_in(x, mod, norm_g, w_in):
    bsz, seq, d = x.shape
    tl = ROW_TILE
    n_in = w_in.shape[1]
    row = lambda width: pl.BlockSpec((1, tl, width), lambda b, i: (b, i, 0))
    return pl.pallas_call(
        _even_in_kernel,
        grid=(bsz, seq // tl),
        in_specs=[row(d),
                  pl.BlockSpec((1, 3, d), lambda b, i: (b, 0, 0)),
                  pl.BlockSpec((1, d), lambda b, i: (0, 0)),
                  pl.BlockSpec((d, n_in), lambda b, i: (0, 0))],
        out_specs=[row(CONV_WIDTH), row(CONV_WIDTH), row(SSM_WIDTH), row(SSM_WIDTH)],
        out_shape=[jax.ShapeDtypeStruct((bsz, seq, CONV_WIDTH), F32),
                   jax.ShapeDtypeStruct((bsz, seq, CONV_WIDTH), F32),
                   jax.ShapeDtypeStruct((bsz, seq, SSM_WIDTH), F32),
                   jax.ShapeDtypeStruct((bsz, seq, SSM_WIDTH), F32)],
        compiler_params=pltpu.CompilerParams(
            dimension_semantics=("parallel", "parallel"), vmem_limit_bytes=VMEM_LIMIT_BYTES),
    )(x, mod, norm_g.reshape(1, d), w_in.astype(BF16))


def _conv_kernel(cur_ref, halo_ref, ga_ref, w_ref, cb_ref, lg_ref, lb_ref, o_ref, buf_ref, sh_ref):
    tl = cur_ref.shape[1]
    first = pl.program_id(1) == 0
    buf_ref[0:CONV_HALO, :] = jnp.where(first, 0.0, halo_ref[0])
    buf_ref[CONV_HALO:CONV_HALO + tl, :] = cur_ref[0]
    lead = CONV_HALO - (CONV_KERNEL - 1)
    rows = sh_ref.shape[1]
    for r in range(1, SUBLANES):
        sh_ref[r - 1] = buf_ref[r:r + rows, :]

    groups = CONV_CHUNK // SUBLANES
    for r0 in range(0, tl, CONV_CHUNK):
        acc = jnp.zeros((groups, SUBLANES, CONV_WIDTH), F32)
        for k in range(CONV_KERNEL):
            r, base = (lead + k) % SUBLANES, r0 + (lead + k) // SUBLANES * SUBLANES
            src = buf_ref if r == 0 else sh_ref.at[r - 1]
            rows8 = src[base:base + CONV_CHUNK, :].reshape(groups, SUBLANES, CONV_WIDTH)
            acc = acc + rows8 * w_ref[k * SUBLANES:(k + 1) * SUBLANES, :][None]
        acc = acc.reshape(CONV_CHUNK, CONV_WIDTH) + cb_ref[...]
        mu = jnp.mean(acc, axis=-1, keepdims=True)
        cen = acc - mu
        var = jnp.mean(cen * cen, axis=-1, keepdims=True)
        y = cen * lax.rsqrt(var + EPS) * lg_ref[...] + lb_ref[...]
        o_ref[0, r0:r0 + CONV_CHUNK, :] = (_silu(y) * ga_ref[0, r0:r0 + CONV_CHUNK, :]).astype(BF16)


def _conv_branch(hg, ga, conv_w, conv_b, ln_g, ln_b):
    bsz, seq, cw = hg.shape
    tl = CONV_TILE
    halo_per_tile = tl // CONV_HALO
    row = pl.BlockSpec((1, tl, cw), lambda b, i: (b, i, 0))
    vec = pl.BlockSpec((1, cw), lambda b, i: (0, 0))
    return pl.pallas_call(
        _conv_kernel,
        grid=(bsz, seq // tl),
        in_specs=[row,
                  pl.BlockSpec((1, CONV_HALO, cw),
                               lambda b, i: (b, jnp.maximum(i * halo_per_tile - 1, 0), 0)),
                  row,
                  pl.BlockSpec((CONV_KERNEL * SUBLANES, cw), lambda b, i: (0, 0)),
                  vec, vec, vec],
        out_specs=row,
        out_shape=jax.ShapeDtypeStruct((bsz, seq, cw), BF16),
        scratch_shapes=[pltpu.VMEM((CONV_HALO + tl, cw), F32),
                        pltpu.VMEM((SUBLANES - 1, CONV_HALO + tl - SUBLANES, cw), F32)],
        compiler_params=pltpu.CompilerParams(dimension_semantics=("parallel", "parallel")),
    )(hg, hg, ga, jnp.repeat(conv_w, SUBLANES, axis=0), conv_b.reshape(1, cw), ln_g.reshape(1, cw),
      ln_b.reshape(1, cw))


def _ssm_prep_kernel(lre_ref, lim_ref, ldt_ref, bre_ref, bim_ref, pre_ref, pim_ref, bbre_ref, bbim_ref):
    lre, lim = lre_ref[...], lim_ref[...]
    dt = jnp.exp(ldt_ref[...])
    mag = jnp.exp(lre * dt)
    are, aim = mag * jnp.cos(lim * dt), mag * jnp.sin(lim * dt)
    nre, nim = are - 1.0, aim
    den = lre * lre + lim * lim
    cre = (nre * lre + nim * lim) / den
    cim = (nim * lre - nre * lim) / den
    bre, bim = bre_ref[...], bim_ref[...]
    bbre_ref[...] = cre * bre - cim * bim
    bbim_ref[...] = cre * bim + cim * bre
    pre_ref[...] = are
    pim_ref[...] = aim


def _ssm_prep(lam_re, lam_im, log_dt, b_re, b_im):
    n = SSM_LANES
    flat = lambda t: t.reshape(1, n)
    to_lanes = lambda t: t.reshape(n, SSM_GROUP).T
    ldt = jnp.broadcast_to(log_dt[:, None], (SSM_GROUPS, SSM_STATE))
    outs = pl.pallas_call(
        _ssm_prep_kernel,
        out_shape=[jax.ShapeDtypeStruct((1, n), F32), jax.ShapeDtypeStruct((1, n), F32),
                   jax.ShapeDtypeStruct((SSM_GROUP, n), F32), jax.ShapeDtypeStruct((SSM_GROUP, n), F32)],
    )(flat(lam_re), flat(lam_im), flat(ldt), to_lanes(b_re), to_lanes(b_im))
    return outs


def _group_mask_in():
    r = jnp.arange(SSM_WIDTH)[:, None] // SSM_GROUP
    c = jnp.arange(SSM_LANES)[None, :] // SSM_STATE
    return r == c


def _block_diag_in(bb):
    return jnp.where(_group_mask_in(), jnp.tile(bb, (SSM_GROUPS, 1)), 0.0).astype(BF16)


def _block_diag_out(cmat):
    per_lane = cmat.transpose(0, 2, 1).reshape(SSM_LANES, SSM_GROUP)
    return jnp.where(_group_mask_in().T, jnp.tile(per_lane, (1, SSM_GROUPS)), 0.0).astype(BF16)


def _permute_rows_f32(perm, x):
    hi = x.astype(BF16)
    rest = x - hi.astype(F32)
    mid = rest.astype(BF16)
    lo = (rest - mid.astype(F32)).astype(BF16)
    return (jnp.dot(perm, hi, preferred_element_type=F32) + jnp.dot(perm, mid, preferred_element_type=F32)
            + jnp.dot(perm, lo, preferred_element_type=F32))


def _ssm_kernel(u_ref, gb_ref, wbre_ref, wbim_ref, pre_ref, pim_ref, wcre_ref, wcim_ref,
                d_ref, wg_ref, bg_ref, o_ref, hre_ref, him_ref, sre_ref, sim_ref):
    bsz, tc, sw = u_ref.shape
    rows = bsz * tc

    @pl.when(pl.program_id(0) == 0)
    def _():
        sre_ref[...] = jnp.zeros_like(sre_ref)
        sim_ref[...] = jnp.zeros_like(sim_ref)

    u = u_ref[...].reshape(rows, sw)
    r_idx = lax.broadcasted_iota(jnp.int32, (rows, rows), 0)
    c_idx = lax.broadcasted_iota(jnp.int32, (rows, rows), 1)
    to_time_major = (c_idx == (r_idx % bsz) * tc + r_idx // bsz).astype(BF16)
    to_batch_major = (c_idx == (r_idx % tc) * bsz + r_idx // tc).astype(BF16)
    ub = jnp.dot(to_time_major, u.astype(BF16), preferred_element_type=F32).astype(BF16)
    hw, hn = SSM_WIDTH // 2, SSM_LANES // 2
    for half in range(2):
        chans, lns = slice(half * hw, (half + 1) * hw), slice(half * hn, (half + 1) * hn)
        hre_ref[:, lns] = jnp.dot(ub[:, chans], wbre_ref[chans, lns], preferred_element_type=F32)
        him_ref[:, lns] = jnp.dot(ub[:, chans], wbim_ref[chans, lns], preferred_element_type=F32)

    width = SSM_SCAN_TILES * LANES
    for j0 in range(0, SSM_LANES, width):
        ls = slice(j0, j0 + width)
        are = jnp.broadcast_to(pre_ref[:, ls], (bsz, width))
        aim = jnp.broadcast_to(pim_ref[:, ls], (bsz, width))

        def tick(t, h):
            hr, hi = h
            r0 = pl.multiple_of(t * bsz, bsz)
            nr = (are * hr - aim * hi) + hre_ref[pl.ds(r0, bsz), ls]
            ni = (are * hi + aim * hr) + him_ref[pl.ds(r0, bsz), ls]
            hre_ref[pl.ds(r0, bsz), ls] = nr
            him_ref[pl.ds(r0, bsz), ls] = ni
            return nr, ni

        hr, hi = lax.fori_loop(0, tc, tick, (sre_ref[:, ls], sim_ref[:, ls]), unroll=SSM_SCAN_UNROLL)
        sre_ref[:, ls] = hr
        sim_ref[:, ls] = hi

    ys = []
    for half in range(2):
        cols, lns = slice(half * hw, (half + 1) * hw), slice(half * hn, (half + 1) * hn)
        ys.append(jnp.dot(hre_ref[:, lns].astype(BF16), wcre_ref[lns, cols], preferred_element_type=F32)
                  - jnp.dot(him_ref[:, lns].astype(BF16), wcim_ref[lns, cols], preferred_element_type=F32))
    y = _permute_rows_f32(to_batch_major, jnp.concatenate(ys, axis=1))
    y = y + d_ref[...] * u
    y = jax.nn.gelu(y)
    y = y * jax.nn.sigmoid(_bdot(y, wg_ref[...]) + bg_ref[...])
    o_ref[...] = (y * gb_ref[...].reshape(rows, sw)).reshape(bsz, tc, sw).astype(BF16)


def _ssm_branch(u, gb, wbre, wbim, pre, pim, wcre, wcim, d_skip, w_glu, b_glu):
    bsz, seq, sw = u.shape
    assert bsz == SUBLANES, "the scan keeps the batch on the sublanes of one vreg tile"
    tc = SSM_CHUNK
    n = SSM_LANES
    row = pl.BlockSpec((bsz, tc, sw), lambda i: (0, i, 0))
    full = lambda r, c: pl.BlockSpec((r, c), lambda i: (0, 0))
    return pl.pallas_call(
        _ssm_kernel,
        grid=(seq // tc,),
        in_specs=[row, row, full(sw, n), full(sw, n), full(1, n), full(1, n),
                  full(n, sw), full(n, sw), full(1, sw), full(sw, sw), full(1, sw)],
        out_specs=row,
        out_shape=jax.ShapeDtypeStruct((bsz, seq, sw), BF16),
        scratch_shapes=[pltpu.VMEM((bsz * tc, n), F32), pltpu.VMEM((bsz * tc, n), F32),
                        pltpu.VMEM((bsz, n), F32), pltpu.VMEM((bsz, n), F32)],
        compiler_params=pltpu.CompilerParams(
            dimension_semantics=("arbitrary",), vmem_limit_bytes=VMEM_LIMIT_BYTES),
    )(u, gb, wbre, wbim, pre, pim, wcre, wcim, d_skip.reshape(1, sw), w_glu.astype(BF16),
      b_glu.reshape(1, sw))


def _mid_kernel(x_ref, ya_ref, yb_ref, mod0_ref, mod1_ref, g_ref, wo_ref, wi_ref,
                x1_ref, q_ref, k_ref, v_ref, sg_ref):
    cw, aw = CONV_WIDTH, ATTN_WIDTH
    tl = x_ref.shape[1]
    mod1 = mod1_ref[0]
    scale = ATTN_HEAD_DIM ** -0.5 * LOG2_E
    halves = [slice(r, r + tl // 2) for r in (0, tl // 2)]
    ys = [jnp.dot(ya_ref[0, rows, :].astype(BF16), wo_ref[0:cw, :], preferred_element_type=F32)
          + jnp.dot(yb_ref[0, rows, :].astype(BF16), wo_ref[cw:, :], preferred_element_type=F32)
          for rows in halves]
    hs = []
    for rows, y in zip(halves, ys):
        x1 = x_ref[0, rows, :] + mod0_ref[0][2:3] * y
        x1_ref[0, rows, :] = x1
        hs.append(_rms_modulate(x1, g_ref[...], mod1[1:2], mod1[0:1]).astype(BF16))
    for rows, h in zip(halves, hs):
        q_ref[0, rows, :] = (jnp.dot(h, wi_ref[:, 0:aw], preferred_element_type=F32) * scale).astype(BF16)
        k_ref[0, rows, :] = jnp.dot(h, wi_ref[:, aw:2 * aw], preferred_element_type=F32).astype(BF16)
        v_ref[0, rows, :] = jnp.dot(h, wi_ref[:, 2 * aw:3 * aw], preferred_element_type=F32).astype(BF16)
        sg_ref[0, rows, :] = _silu(jnp.dot(h, wi_ref[:, 3 * aw:4 * aw], preferred_element_type=F32)).astype(BF16)


def _mid(x, ya, yb, mod0, mod1, norm_g, w_out, w_in):
    bsz, seq, d = x.shape
    tl = ROW_TILE
    row = lambda width: pl.BlockSpec((1, tl, width), lambda b, i: (b, i, 0))
    modspec = pl.BlockSpec((1, 3, d), lambda b, i: (b, 0, 0))
    full = lambda r, c: pl.BlockSpec((r, c), lambda b, i: (0, 0))
    aw = ATTN_WIDTH
    return pl.pallas_call(
        _mid_kernel,
        grid=(bsz, seq // tl),
        in_specs=[row(d), row(CONV_WIDTH), row(SSM_WIDTH), modspec, modspec, full(1, d),
                  full(CONV_WIDTH + SSM_WIDTH, d), full(d, 4 * aw)],
        out_specs=[row(d), row(aw), row(aw), row(aw), row(aw)],
        out_shape=[jax.ShapeDtypeStruct((bsz, seq, d), F32),
                   jax.ShapeDtypeStruct((bsz, seq, aw), BF16),
                   jax.ShapeDtypeStruct((bsz, seq, aw), BF16),
                   jax.ShapeDtypeStruct((bsz, seq, aw), BF16),
                   jax.ShapeDtypeStruct((bsz, seq, aw), BF16)],
        compiler_params=pltpu.CompilerParams(
            dimension_semantics=("parallel", "parallel"), vmem_limit_bytes=VMEM_LIMIT_BYTES),
    )(x, ya, yb, mod0, mod1, norm_g.reshape(1, d), w_out.astype(BF16), w_in.astype(BF16))


def _attn_schedule(nq):
    visits = []
    for q in range(nq):
        visits.append((q, q, False))
        visits.append((q, q - 1, False) if q >= 1 else (0, 0, True))
    for q in range(2, nq):
        visits.extend((q, j, False) for j in range(q - 2, -1, -1))
    visits.append((0, 0, True))
    if len(visits) % 2:
        visits.append((0, 0, True))
    rows = []
    prev = (0, 0, True)
    for q, j, dummy in visits:
        pq, pj, pdummy = prev
        rows.append((q, j, nq if pdummy else pq, pj, int(pdummy)))
        prev = (q, j, dummy)
    return np.asarray(rows, np.int32).T.copy(), 2 * nq, len(visits)


def _attn_kernel(tbl_ref, q_ref, k_ref, v_ref, sg_ref, o_ref, acc_ref, carry_ref, qn_ref, qp_ref,
                 sp0_ref, sp1_ref, e0_ref, e1_ref, f0_ref, f1_ref, *, diag_steps, total_steps):
    blk = ATTN_BLOCK
    pair = 2 * ATTN_HEAD_DIM
    heads = range(2 * ATTN_PAIRS)
    cols = [slice((s // 2) * pair, (s // 2 + 1) * pair) for s in heads]
    nq = q_ref.shape[1] // blk
    lane = lax.broadcasted_iota(jnp.int32, (blk, pair), 1)
    first_head = lane < ATTN_HEAD_DIM
    r_idx = lax.broadcasted_iota(jnp.int32, (blk, blk), 0)
    c_idx = lax.broadcasted_iota(jnp.int32, (blk, blk), 1)
    after = (r_idx > c_idx).astype(BF16)
    causal = c_idx < r_idx
    nt = (((1,), (1,)), ((), ()))

    for qb in range(nq):
        for s in heads:
            q = q_ref[0, qb * blk:(qb + 1) * blk, cols[s]].astype(F32)
            own = first_head == (s % 2 == 0)
            qn_ref[qb, s] = jnp.where(own, -q, 0.0).astype(BF16)
            qp_ref[qb, s] = jnp.where(own, q, 0.0).astype(BF16)
    acc_ref[...] = jnp.zeros_like(acc_ref)
    carry_ref[...] = jnp.zeros_like(carry_ref)
    sp1_ref[...] = jnp.zeros_like(sp1_ref)
    e1_ref[...] = jnp.zeros_like(e1_ref)
    f1_ref[...] = jnp.zeros_like(f1_ref)

    def step(i, diagonal, sp_a, e_a, first_a, sp_b, e_b, first_b):
        a_q, a_key = tbl_ref[0, i], tbl_ref[1, i]
        b_slot, b_key, b_dummy = tbl_ref[2, i], tbl_ref[3, i], tbl_ref[4, i]
        a_start = pl.multiple_of(a_key * blk, blk)
        b_start = pl.multiple_of(b_key * blk, blk)
        penalty = jnp.where(b_dummy == 1, -MASKED_LOGIT, 0.0)
        sums = [jnp.dot(sp_b[s], after, preferred_element_type=F32) for s in heads]
        zs = [(lax.dot_general(qn_ref[a_q, s], k_ref[0, pl.ds(a_start, blk), cols[s]], nt,
                               preferred_element_type=F32),
               lax.dot_general(qp_ref[a_q, s], k_ref[0, pl.ds(a_start, blk), cols[s]], nt,
                               preferred_element_type=F32)) for s in heads]
        for s in heads:
            zn, zp = zs[s]
            l = jnp.minimum(zn, 0.0) - jnp.log(1.0 + jnp.exp2(jnp.minimum(zn, zp))) * LOG2_E
            neg_log_beta = zn - l
            if diagonal:
                l = jnp.where(causal, l, 0.0)
                neg_log_beta = jnp.where(causal, neg_log_beta, MASKED_LOGIT)
            sp_a[s] = l.astype(BF16)
            e_a[s] = neg_log_beta
            first_a[s] = l[:, 0:1]
        for s in heads:
            w = jnp.exp2((sums[s] + (carry_ref[b_slot, s] + penalty)) - e_b[s])
            acc_ref[b_slot, s] += jnp.dot(w.astype(BF16), v_ref[0, pl.ds(b_start, blk), cols[s]],
                                          preferred_element_type=F32)
            carry_ref[b_slot, s] += sums[s][:, 0:1] + first_b[s]

    def diag_pair(p, _):
        step(2 * p, True, sp0_ref, e0_ref, f0_ref, sp1_ref, e1_ref, f1_ref)
        step(2 * p + 1, False, sp1_ref, e1_ref, f1_ref, sp0_ref, e0_ref, f0_ref)
        return 0

    def plain_pair(p, _):
        step(2 * p, False, sp0_ref, e0_ref, f0_ref, sp1_ref, e1_ref, f1_ref)
        step(2 * p + 1, False, sp1_ref, e1_ref, f1_ref, sp0_ref, e0_ref, f0_ref)
        return 0

    lax.fori_loop(0, diag_steps // 2, diag_pair, 0)
    lax.fori_loop(diag_steps // 2, total_steps // 2, plain_pair, 0)
    for qb in range(nq):
        for p in range(ATTN_PAIRS):
            rows = slice(qb * blk, (qb + 1) * blk)
            gated = (jnp.where(first_head, acc_ref[qb, 2 * p], acc_ref[qb, 2 * p + 1])
                     * sg_ref[0, rows, cols[2 * p]].astype(F32))
            o_ref[0, rows, cols[2 * p]] = gated.astype(o_ref.dtype)


def _attention(q, k, v, sg):
    bsz, seq, aw = q.shape
    pair = 2 * ATTN_HEAD_DIM
    nh = 2 * ATTN_PAIRS
    width = ATTN_PAIRS * pair
    blk = ATTN_BLOCK
    nq = seq // blk
    table, diag_steps, total_steps = _attn_schedule(nq)
    assert total_steps % 2 == 0 and table.shape[1] == total_steps
    seq_block = pl.BlockSpec((1, seq, width), lambda b, h, tbl: (b, 0, h))
    return pl.pallas_call(
        functools.partial(_attn_kernel, diag_steps=diag_steps, total_steps=total_steps),
        grid_spec=pltpu.PrefetchScalarGridSpec(
            num_scalar_prefetch=1,
            grid=(bsz, aw // width),
            in_specs=[seq_block, seq_block, seq_block, seq_block],
            out_specs=seq_block,
            scratch_shapes=[pltpu.VMEM((nq + 1, nh, blk, pair), F32),
                            pltpu.VMEM((nq + 1, nh, blk, 1), F32),
                            pltpu.VMEM((nq, nh, blk, pair), BF16),
                            pltpu.VMEM((nq, nh, blk, pair), BF16),
                            pltpu.VMEM((nh, blk, blk), BF16), pltpu.VMEM((nh, blk, blk), BF16),
                            pltpu.VMEM((nh, blk, blk), F32), pltpu.VMEM((nh, blk, blk), F32),
                            pltpu.VMEM((nh, blk, 1), F32), pltpu.VMEM((nh, blk, 1), F32)]),
        out_shape=jax.ShapeDtypeStruct((bsz, seq, aw), BF16),
        compiler_params=pltpu.CompilerParams(
            dimension_semantics=("parallel", "parallel"), vmem_limit_bytes=VMEM_LIMIT_BYTES),
    )(jnp.asarray(table), q, k, v, sg)


def _final_kernel(x_ref, o_ref, mod_ref, wo_ref, g_ref, out_ref):
    y = jnp.dot(o_ref[0], wo_ref[...], preferred_element_type=F32)
    x2 = x_ref[0] + mod_ref[0][2:3] * y
    ms = jnp.mean(x2 * x2, axis=-1, keepdims=True)
    out_ref[0] = x2 * lax.rsqrt(ms + EPS) * g_ref[...]


def _final(x1, o, mod1, w_out, final_g):
    bsz, seq, d = x1.shape
    tl = ROW_TILE
    row = pl.BlockSpec((1, tl, d), lambda b, i: (b, i, 0))
    return pl.pallas_call(
        _final_kernel,
        grid=(bsz, seq // tl),
        in_specs=[row, row,
                  pl.BlockSpec((1, 3, d), lambda b, i: (b, 0, 0)),
                  pl.BlockSpec((ATTN_WIDTH, d), lambda b, i: (0, 0)),
                  pl.BlockSpec((1, d), lambda b, i: (0, 0))],
        out_specs=row,
        out_shape=jax.ShapeDtypeStruct((bsz, seq, d), F32),
        compiler_params=pltpu.CompilerParams(dimension_semantics=("parallel", "parallel")),
    )(x1, o, mod1, w_out.astype(BF16), final_g.reshape(1, d))


def kernel(x, c, l0_norm_g, l0_w_ada, l0_b_ada, l0_w_in, l0_conv_w, l0_conv_b, l0_conv_ln_g, l0_conv_ln_b, l0_ssm_lam_re, l0_ssm_lam_im, l0_ssm_log_dt, l0_ssm_b_re, l0_ssm_b_im, l0_ssm_c_re, l0_ssm_c_im, l0_ssm_d, l0_ssm_w_glu, l0_ssm_b_glu, l0_w_out, l1_norm_g, l1_w_ada, l1_b_ada, l1_w_in, l1_w_out, final_norm_g):
    bsz, seq, d = x.shape
    mod0 = _ada(c, l0_w_ada, l0_b_ada).reshape(bsz, 3, d)
    mod1 = _ada(c, l1_w_ada, l1_b_ada).reshape(bsz, 3, d)

    hg, ga, u, gb = _even_in(x, mod0, l0_norm_g, l0_w_in)
    ya = _conv_branch(hg, ga, l0_conv_w, l0_conv_b, l0_conv_ln_g, l0_conv_ln_b)

    pre, pim, bbre, bbim = _ssm_prep(l0_ssm_lam_re, l0_ssm_lam_im, l0_ssm_log_dt, l0_ssm_b_re, l0_ssm_b_im)
    yb = _ssm_branch(u, gb, _block_diag_in(bbre), _block_diag_in(bbim), pre, pim,
                     _block_diag_out(l0_ssm_c_re), _block_diag_out(l0_ssm_c_im),
                     l0_ssm_d, l0_ssm_w_glu, l0_ssm_b_glu)

    x1, q, k, v, sg = _mid(x, ya, yb, mod0, mod1, l1_norm_g, l0_w_out, l1_w_in)
    o = _attention(q, k, v, sg)
    return _final(x1, o, mod1, l1_w_out, final_norm_g)
```
